```python
import jax, jax.numpy as jnp
from jax import lax
import numpy as np

D_MODEL = 1024
BATCH = 4
SEQ = 4096
DEPTH = 4
DEC_BATCH = 128
DEC_SEQ = 4
PAST_LEN = 8192
PAGE_SIZE = 128

HEAD_DIM = 64
A_HEADS = 4
A_KV_HEADS = 2
A_WINDOW = 128
B_HEADS = 6
B_PATTERNS = ((128, 1), (512, 4), (2048, 16))
B_WINDOW_MAX = 2048
C_HEADS = 6
DECAY_LORA = 64
AAA_LORA = 64
GATE_LORA = 128
D_FF = 4 * D_MODEL
Q_BLOCK = 128
RMS_EPS = 1e-5
GN_EPS = 64e-5

A_Q = A_HEADS * HEAD_DIM
A_KV = A_KV_HEADS * HEAD_DIM
B_W = B_HEADS * HEAD_DIM
C_W = C_HEADS * HEAD_DIM
C_PROJ = 3 * C_W + DECAY_LORA + AAA_LORA + GATE_LORA
IN_COLS = A_Q + 2 * A_KV + 3 * B_W + C_PROJ
MIX_W = A_Q + B_W + C_W
IN_SPLITS = (A_Q, A_Q + A_KV, A_Q + 2 * A_KV, A_Q + 2 * A_KV + B_W,
             A_Q + 2 * A_KV + 2 * B_W, A_Q + 2 * A_KV + 3 * B_W)
C_SPLITS = (C_W, 2 * C_W, 3 * C_W, 3 * C_W + DECAY_LORA, 3 * C_W + DECAY_LORA + AAA_LORA)

kernel_name = 'hybrid_swa_sink_dilated_rwkv7_step'


def rmsnorm(x, g):
    xf = x.astype(jnp.float32)
    y = xf * lax.rsqrt(jnp.mean(xf * xf, axis=-1, keepdims=True) + RMS_EPS)
    return (y * g.astype(jnp.float32)).astype(x.dtype)


def banded_attn(q, k, v, n_past, window, sink=None):
    N, L, H, hd = q.shape
    G = k.shape[2]
    rep = H // G
    blk = L if L <= Q_BLOCK else Q_BLOCK
    nb = -(-L // blk)
    Lp = nb * blk
    front = window - n_past
    kv_pad = ((0, 0), (front, Lp - L), (0, 0), (0, 0))
    k = jnp.pad(k, kv_pad)
    v = jnp.pad(v, kv_pad)
    q = jnp.pad(q, ((0, 0), (0, Lp - L), (0, 0), (0, 0))).reshape(N, nb, blk, G, rep, hd)
    span = window + blk
    kidx = jnp.arange(nb)[:, None] * blk + jnp.arange(span)[None, :]
    kb = k[:, kidx]
    vb = v[:, kidx]
    s = jnp.einsum('nbqgrd,nbkgd->nbgrqk', q, kb).astype(jnp.float32) * (hd ** -0.5)
    dist = jnp.arange(blk)[:, None] + window - jnp.arange(span)[None, :]
    mask = ((dist >= 0) & (dist <= window))[None] & (kidx >= front)[:, None, :]
    s = jnp.where(mask[None, :, None, None], s, -jnp.inf)
    m = jnp.max(s, axis=-1)
    if sink is not None:
        sk = sink.astype(jnp.float32).reshape(1, 1, G, rep, 1)
        m = jnp.maximum(m, sk)
    p = jnp.exp(s - m[..., None])
    den = jnp.sum(p, axis=-1)
    if sink is not None:
        den = den + jnp.exp(sk - m)
    o = jnp.einsum('nbgrqk,nbkgd->nbqgrd', p.astype(v.dtype), vb).astype(jnp.float32)
    den_q = jnp.transpose(den, (0, 1, 4, 2, 3))
    o = (o / den_q[..., None]).reshape(N, Lp, H, hd)[:, :L]
    lse = (jnp.transpose(m, (0, 1, 4, 2, 3)) + jnp.log(den_q)).reshape(N, Lp, H)[:, :L]
    return o, lse


def merge_patterns(outs, lses):
    wts = jax.nn.softmax(jnp.stack(lses), axis=0)
    return jnp.sum(wts[..., None] * jnp.stack(outs), axis=0)


def dilated_prompt(q, k, v):
    N, S, H, hd = q.shape
    outs, lses = [], []
    for win, d in B_PATTERNS:
        Ls = S // d
        def to_classes(t):
            return t.reshape(N, Ls, d, H, hd).transpose(0, 2, 1, 3, 4).reshape(N * d, Ls, H, hd)
        o, lse = banded_attn(to_classes(q), to_classes(k), to_classes(v), 0, win // d)
        outs.append(o.reshape(N, d, Ls, H, hd).transpose(0, 2, 1, 3, 4).reshape(N, S, H, hd))
        lses.append(lse.reshape(N, d, Ls, H).transpose(0, 2, 1, 3).reshape(N, S, H))
    return merge_patterns(outs, lses)


def dilated_sample(q, k_all, v_all):
    N, L, H, hd = q.shape
    n_past = k_all.shape[1] - L
    outs, lses = [], []
    for win, d in B_PATTERNS:
        steps = jnp.arange(win // d + 1)
        idx = n_past + jnp.arange(L)[:, None] - steps[None, :] * d
        valid = idx >= 0
        idx = jnp.maximum(idx, 0)
        kg = k_all[:, idx]
        vg = v_all[:, idx]
        s = jnp.einsum('nlhd,nlkhd->nlhk', q, kg).astype(jnp.float32) * (hd ** -0.5)
        s = jnp.where(valid[None, :, None, :], s, -jnp.inf)
        m = jnp.max(s, axis=-1)
        p = jnp.exp(s - m[..., None])
        den = jnp.sum(p, axis=-1)
        o = jnp.einsum('nlhk,nlkhd->nlhd', p.astype(vg.dtype), vg).astype(jnp.float32) / den[..., None]
        outs.append(o)
        lses.append(m + jnp.log(den))
    return merge_patterns(outs, lses)


def wkv_scan(s0, r, w, k, v, kk, a):
    def step(S, inp):
        r_t, w_t, k_t, v_t, kk_t, a_t = inp
        sa = jnp.einsum('nhij,nhj->nhi', S, -kk_t)
        S = S * w_t[:, :, None, :] + sa[..., None] * (kk_t * a_t)[:, :, None, :] + v_t[..., None] * k_t[:, :, None, :]
        return S, jnp.einsum('nhij,nhj->nhi', S, r_t)
    xs = tuple(jnp.swapaxes(t, 0, 1) for t in (r, w, k, v, kk, a))
    s_fin, y = lax.scan(step, s0, xs)
    return jnp.swapaxes(y, 0, 1), s_fin


def rwkv_mixer(zc, z_before, s0, w):
    N, L, _ = zc.shape
    f32 = jnp.float32
    z_prev = jnp.concatenate([z_before[:, None, :].astype(zc.dtype), zc[:, :-1]], axis=1)
    zs = (zc + (z_prev - zc) * w['c_mu'].astype(zc.dtype)).astype(f32)
    r, k, v, zw, za, zg = jnp.split(zs, C_SPLITS, axis=-1)
    w_log = -jax.nn.softplus(-(w['c_w0'].astype(f32) + jnp.tanh(zw) @ w['c_w2'].astype(f32))) - 0.5
    decay = jnp.exp(-jnp.exp(w_log))
    a = jax.nn.sigmoid(w['c_a0'].astype(f32) + za @ w['c_a2'].astype(f32))
    g = jax.nn.sigmoid(zg) @ w['c_g2'].astype(f32)
    heads = lambda t: t.reshape(N, L, C_HEADS, HEAD_DIM)
    kk = heads(k * w['c_k_k'].astype(f32))
    kk = kk / jnp.maximum(jnp.sqrt(jnp.sum(kk * kk, axis=-1, keepdims=True)), 1e-12)
    k = k * (1.0 + (a - 1.0) * w['c_k_a'].astype(f32))
    rh, kh, vh, ah, dh = heads(r), heads(k), heads(v), heads(a), heads(decay)
    y, s_fin = wkv_scan(s0.astype(f32), rh, dh, kh, vh, kk, ah)
    mu = jnp.mean(y, axis=-1, keepdims=True)
    var = jnp.mean(jnp.square(y - mu), axis=-1, keepdims=True)
    ln_w = w['c_ln_w'].astype(f32).reshape(C_HEADS, HEAD_DIM)
    ln_b = w['c_ln_b'].astype(f32).reshape(C_HEADS, HEAD_DIM)
    y = (y - mu) * lax.rsqrt(var + GN_EPS) * ln_w + ln_b
    y = y + jnp.sum(rh * kh * w['c_r_k'].astype(f32), axis=-1, keepdims=True) * vh
    return y.reshape(N, L, C_W) * g, s_fin


def trunk_layer(x, w, past):
    N, L, _ = x.shape
    h = rmsnorm(x, w['norm1_g'])
    proj = jnp.einsum('nld,dc->nlc', h, w['w_in'])
    qa, ka, va, qb, kb, vb, zc = jnp.split(proj, IN_SPLITS, axis=-1)
    qa = qa.reshape(N, L, A_HEADS, HEAD_DIM)
    ka = ka.reshape(N, L, A_KV_HEADS, HEAD_DIM)
    va = va.reshape(N, L, A_KV_HEADS, HEAD_DIM)
    qb = qb.reshape(N, L, B_HEADS, HEAD_DIM)
    kb = kb.reshape(N, L, B_HEADS, HEAD_DIM)
    vb = vb.reshape(N, L, B_HEADS, HEAD_DIM)
    if past is None:
        oa, _ = banded_attn(qa, ka, va, 0, A_WINDOW, w['a_sink'])
        ob = dilated_prompt(qb, kb, vb)
        z_before = jnp.zeros((N, C_PROJ), zc.dtype)
        s0 = jnp.zeros((N, C_HEADS, HEAD_DIM, HEAD_DIM), jnp.float32)
    else:
        ca_k, ca_v, cb_k, cb_v, s0, z_before = past
        ka_all = jnp.concatenate([ca_k.astype(ka.dtype), ka], axis=1)
        va_all = jnp.concatenate([ca_v.astype(va.dtype), va], axis=1)
        oa, _ = banded_attn(qa, ka_all, va_all, ca_k.shape[1], A_WINDOW, w['a_sink'])
        kb_all = jnp.concatenate([cb_k.astype(kb.dtype), kb], axis=1)
        vb_all = jnp.concatenate([cb_v.astype(vb.dtype), vb], axis=1)
        ob = dilated_sample(qb, kb_all, vb_all)
    oc, s_fin = rwkv_mixer(zc, z_before, s0, w)
    mix = jnp.concatenate([oa.reshape(N, L, A_Q), ob.reshape(N, L, B_W), oc], axis=-1).astype(x.dtype)
    x = x + jnp.einsum('nlc,cd->nld', mix, w['w_out'])
    u = jnp.einsum('nld,df->nlf', rmsnorm(x, w['norm2_g']), w['mlp_up'])
    x = x + jnp.einsum('nlf,fd->nld', jnp.square(jax.nn.relu(u)), w['mlp_down'])
    if past is None:
        ra = min(A_WINDOW, L)
        rb = min(B_WINDOW_MAX, L)
        new = (ka[:, L - ra:], va[:, L - ra:], kb[:, L - rb:], vb[:, L - rb:], s_fin, zc[:, -1])
    else:
        new = (ka, va, kb, vb, s_fin, zc[:, -1])
    return x, new


def setup_inputs(seed: int = 0) -> dict:
    key = jax.random.key(seed)
    ks = jax.random.split(key, 32)
    f32 = jnp.float32
    nrm = lambda kk, shape, scale: jax.random.normal(kk, shape, f32) * scale
    pa = min(A_WINDOW, PAST_LEN)
    pb = min(B_WINDOW_MAX, PAST_LEN)
    return {
        'x_prompt': nrm(ks[0], (BATCH, SEQ, D_MODEL), 1.0),
        'x_sample': nrm(ks[1], (DEC_BATCH, DEC_SEQ, D_MODEL), 1.0),
        'cache_a_k': nrm(ks[2], (DEPTH, DEC_BATCH, pa, A_KV_HEADS, HEAD_DIM), 1.0),
        'cache_a_v': nrm(ks[3], (DEPTH, DEC_BATCH, pa, A_KV_HEADS, HEAD_DIM), 1.0),
        'cache_b_k': nrm(ks[4], (DEPTH, DEC_BATCH, pb, B_HEADS, HEAD_DIM), 1.0),
        'cache_b_v': nrm(ks[5], (DEPTH, DEC_BATCH, pb, B_HEADS, HEAD_DIM), 1.0),
        'state_c_wkv': nrm(ks[6], (DEPTH, DEC_BATCH, C_HEADS, HEAD_DIM, HEAD_DIM), 0.3),
        'state_c_shift': nrm(ks[7], (DEPTH, DEC_BATCH, C_PROJ), 1.0),
        'norm1_g': 1.0 + nrm(ks[8], (DEPTH, D_MODEL), 0.02),
        'norm2_g': 1.0 + nrm(ks[9], (DEPTH, D_MODEL), 0.02),
        'w_in': nrm(ks[10], (DEPTH, D_MODEL, IN_COLS), D_MODEL ** -0.5),
        'w_out': nrm(ks[11], (DEPTH, MIX_W, D_MODEL), MIX_W ** -0.5),
        'a_sink': nrm(ks[12], (DEPTH, A_HEADS), 0.5),
        'c_mu': jax.random.uniform(ks[13], (DEPTH, C_PROJ), f32),
        'c_w0': jax.random.uniform(ks[14], (DEPTH, C_W), f32, -4.0, 1.0),
        'c_w2': nrm(ks[15], (DEPTH, DECAY_LORA, C_W), 0.5 * DECAY_LORA ** -0.5),
        'c_a0': nrm(ks[16], (DEPTH, C_W), 0.5),
        'c_a2': nrm(ks[17], (DEPTH, AAA_LORA, C_W), 0.5 * AAA_LORA ** -0.5),
        'c_g2': nrm(ks[18], (DEPTH, GATE_LORA, C_W), GATE_LORA ** -0.5),
        'c_k_k': 0.85 + nrm(ks[19], (DEPTH, C_W), 0.05),
        'c_k_a': 1.0 + nrm(ks[20], (DEPTH, C_W), 0.05),
        'c_r_k': nrm(ks[21], (DEPTH, C_HEADS, HEAD_DIM), 0.1),
        'c_ln_w': 1.0 + nrm(ks[22], (DEPTH, C_W), 0.02),
        'c_ln_b': nrm(ks[23], (DEPTH, C_W), 0.02),
        'mlp_up': nrm(ks[24], (DEPTH, D_MODEL, D_FF), D_MODEL ** -0.5),
        'mlp_down': nrm(ks[25], (DEPTH, D_FF, D_MODEL), D_FF ** -0.5),
        'final_norm_g': 1.0 + nrm(ks[26], (D_MODEL,), 0.02),
    }


def reference(x_prompt, x_sample, cache_a_k, cache_a_v, cache_b_k, cache_b_v, state_c_wkv, state_c_shift,
              norm1_g, norm2_g, w_in, w_out, a_sink, c_mu, c_w0, c_w2, c_a0, c_a2, c_g2, c_k_k, c_k_a, c_r_k,
              c_ln_w, c_ln_b, mlp_up, mlp_down, final_norm_g):
    stacked = {'norm1_g': norm1_g, 'norm2_g': norm2_g, 'w_in': w_in, 'w_out': w_out, 'a_sink': a_sink,
               'c_mu': c_mu, 'c_w0': c_w0, 'c_w2': c_w2, 'c_a0': c_a0, 'c_a2': c_a2, 'c_g2': c_g2,
               'c_k_k': c_k_k, 'c_k_a': c_k_a, 'c_r_k': c_r_k, 'c_ln_w': c_ln_w, 'c_ln_b': c_ln_b,
               'mlp_up': mlp_up, 'mlp_down': mlp_down}
    yp, ys = x_prompt, x_sample
    p_new, s_new = [], []
    for l in range(DEPTH):
        w = {name: arr[l] for name, arr in stacked.items()}
        yp, st_p = trunk_layer(yp, w, None)
        ys, st_s = trunk_layer(ys, w, (cache_a_k[l], cache_a_v[l], cache_b_k[l], cache_b_v[l],
                                       state_c_wkv[l], state_c_shift[l]))
        p_new.append(st_p)
        s_new.append(st_s)
    p_a_k, p_a_v, p_b_k, p_b_v, p_c_wkv, p_c_shift = [jnp.stack(t) for t in zip(*p_new)]
    s_a_k, s_a_v, s_b_k, s_b_v, s_c_wkv, s_c_shift = [jnp.stack(t) for t in zip(*s_new)]
    y_prompt = rmsnorm(yp, final_norm_g)
    y_sample = rmsnorm(ys, final_norm_g)
    return (y_prompt, y_sample, p_a_k, p_a_v, p_b_k, p_b_v, p_c_wkv, p_c_shift,
            s_a_k, s_a_v, s_b_k, s_b_v, s_c_wkv, s_c_shift)
```

```python
import functools

import jax
import jax.numpy as jnp
from jax import lax
from jax.experimental import pallas as pl
from jax.experimental.pallas import tpu as pltpu

F32 = jnp.float32
BF16 = jnp.bfloat16
HI = lax.Precision.HIGHEST

LANES = 128
HEAD_DIM = 64
BAND = 128
CHUNK = 64
RMS_EPS = 1e-5
GN_EPS = 64e-5
NEG = -1e30
VMEM_LIMIT = 56 * 1024 * 1024

A_Q, A_KV, B_W, C_W = 256, 128, 384, 384
C_PROJ = 1408
SPLITS = (0, 256, 384, 512, 896, 1280, 1664, 3072)
B_DILATIONS = (1, 4, 16)


def _cparams(sem):
    return pltpu.CompilerParams(dimension_semantics=sem, vmem_limit_bytes=VMEM_LIMIT)


def _dot(a, b):
    return jnp.dot(a.astype(BF16), b.astype(BF16), preferred_element_type=F32)


def _dot_nt(a, b):
    return lax.dot_general(a.astype(BF16), b.astype(BF16), (((1,), (1,)), ((), ())),
                           preferred_element_type=F32)


def _dot_hi(a, b):
    return jnp.dot(a, b, preferred_element_type=F32, precision=HI)


def _iota(shape, dim):
    return lax.broadcasted_iota(jnp.int32, shape, dim)


def _norm_proj_kernel(x_ref, g_ref, w_ref, *out_refs):
    x = x_ref[...]
    h = x * lax.rsqrt(jnp.mean(x * x, axis=-1, keepdims=True) + RMS_EPS) * g_ref[...]
    hb = h.astype(BF16)
    for o_ref, a, b in zip(out_refs, SPLITS[:-1], SPLITS[1:]):
        o_ref[...] = jnp.dot(hb, w_ref[:, a:b], preferred_element_type=F32)


def norm_proj(x, g, w_bf16, tm):
    t, d = x.shape
    widths = [b - a for a, b in zip(SPLITS[:-1], SPLITS[1:])]
    return pl.pallas_call(
        _norm_proj_kernel,
        grid=(t // tm,),
        in_specs=[pl.BlockSpec((tm, d), lambda i: (i, 0)),
                  pl.BlockSpec((1, d), lambda i: (0, 0)),
                  pl.BlockSpec(w_bf16.shape, lambda i: (0, 0))],
        out_specs=[pl.BlockSpec((tm, w), lambda i: (i, 0)) for w in widths],
        out_shape=[jax.ShapeDtypeStruct((t, w), F32) for w in widths],
        compiler_params=_cparams(("parallel",)),
        name="norm_proj",
    )(x, g.reshape(1, d), w_bf16)


def _band_attn_kernel(*refs, nq, kv_of_q, has_sink, emit_lse):
    q_ref, kp_ref, kc_ref, vp_ref, vc_ref = refs[:5]
    pos = 5
    sink_ref = None
    if has_sink:
        sink_ref = refs[pos]
        pos += 1
    o_ref = refs[pos]
    lse_ref = refs[pos + 1] if emit_lse else None

    i = pl.program_id(2)
    qi = _iota((BAND, 2 * BAND), 0)
    kj = _iota((BAND, 2 * BAND), 1)
    mask = (kj >= qi) & (kj <= qi + BAND) & ((kj >= BAND) | (i > 0))
    lane = _iota((BAND, LANES), 1)
    lo = lane < HEAD_DIM
    scale = HEAD_DIM ** -0.5
    for t in range(nq):
        qs = slice(t * LANES, (t + 1) * LANES)
        ks = slice(kv_of_q[t] * LANES, (kv_of_q[t] + 1) * LANES)
        q = q_ref[:, qs]
        k = jnp.concatenate([kp_ref[:, ks], kc_ref[:, ks]], axis=0).astype(BF16)
        v = jnp.concatenate([vp_ref[:, ks], vc_ref[:, ks]], axis=0).astype(BF16)
        outs, lses = [], []
        for half in range(2):
            sel = lo if half == 0 else jnp.logical_not(lo)
            qm = jnp.where(sel, q, 0.0)
            s = _dot_nt(qm, k) * scale
            s = jnp.where(mask, s, NEG)
            m = jnp.max(s, axis=-1, keepdims=True)
            if has_sink:
                c = t * LANES + half * HEAD_DIM
                sk = sink_ref[0:1, c:c + 1]
                m = jnp.maximum(m, sk)
            p = jnp.exp(s - m)
            den = jnp.sum(p, axis=-1, keepdims=True)
            if has_sink:
                den = den + jnp.exp(sk - m)
            outs.append(_dot(p, v) / den)
            lses.append(m + jnp.log(den))
        o_ref[:, qs] = jnp.where(lo, outs[0], outs[1])
        if emit_lse:
            lse_ref[:, qs] = jnp.where(lo, lses[0], lses[1])


def band_attn(q, k, v, *, n, d, kv_of_q, sink=None, emit_lse=False):
    t, wq = q.shape
    wkv = k.shape[1]
    nq = wq // LANES
    nb = t // n // d // BAND
    qv = q.reshape(t // d, d * wq)
    kv = k.reshape(t // d, d * wkv)
    vv = v.reshape(t // d, d * wkv)
    cur = lambda b, r, i: (b * nb + i, r)
    prev = lambda b, r, i: (b * nb + jnp.maximum(i - 1, 0), r)
    in_specs = [pl.BlockSpec((BAND, wq), cur),
                pl.BlockSpec((BAND, wkv), prev), pl.BlockSpec((BAND, wkv), cur),
                pl.BlockSpec((BAND, wkv), prev), pl.BlockSpec((BAND, wkv), cur)]
    args = [qv, kv, kv, vv, vv]
    if sink is not None:
        in_specs.append(pl.BlockSpec((1, wq), lambda b, r, i: (0, 0)))
        args.append(sink)
    n_out = 2 if emit_lse else 1
    outs = pl.pallas_call(
        functools.partial(_band_attn_kernel, nq=nq, kv_of_q=kv_of_q, has_sink=sink is not None,
                          emit_lse=emit_lse),
        grid=(n, d, nb),
        in_specs=in_specs,
        out_specs=[pl.BlockSpec((BAND, wq), cur)] * n_out,
        out_shape=[jax.ShapeDtypeStruct((t // d, d * wq), F32)] * n_out,
        compiler_params=_cparams(("parallel", "parallel", "arbitrary")),
        name="band_attn_d%d" % d,
    )(*args)
    return [o.reshape(t, wq) for o in outs]


def _merge_kernel(o1, o2, o3, l1, l2, l3, out_ref):
    a, b, c = l1[...], l2[...], l3[...]
    m = jnp.maximum(jnp.maximum(a, b), c)
    wa, wb, wc = jnp.exp(a - m), jnp.exp(b - m), jnp.exp(c - m)
    out_ref[...] = (wa * o1[...] + wb * o2[...] + wc * o3[...]) / (wa + wb + wc)


def merge_patterns(outs, lses, tm):
    t, w = outs[0].shape
    spec = pl.BlockSpec((tm, w), lambda i: (i, 0))
    return pl.pallas_call(
        _merge_kernel, grid=(t // tm,), in_specs=[spec] * 6, out_specs=spec,
        out_shape=jax.ShapeDtypeStruct((t, w), F32),
        compiler_params=_cparams(("parallel",)), name="merge_patterns",
    )(*outs, *lses)


NEW_PAD = 8


def _attend_rows(q_row, k_all, v_all, valid, seg_ones, sink_row):
    s = _dot(k_all * q_row, seg_ones) * (HEAD_DIM ** -0.5)
    s = jnp.where(valid, s, NEG)
    m = jnp.max(s, axis=0, keepdims=True)
    if sink_row is not None:
        m = jnp.maximum(m, sink_row)
    p = jnp.exp(s - m)
    den = jnp.sum(p, axis=0, keepdims=True)
    if sink_row is not None:
        den = den + jnp.exp(sink_row - m)
    o = jnp.sum(p * v_all, axis=0, keepdims=True) / den
    return o, m + jnp.log(den)


def _sample_attn_kernel(qa_ref, kan_ref, van_ref, cak_ref, cav_ref, sink_ref, ea_ref,
                        qb_ref, kbn_ref, vbn_ref, ck1_ref, cv1_ref, ck4_ref, cv4_ref, ck16_ref, cv16_ref,
                        eb_ref, oa_ref, ob_ref, *, n_new):
    rows = BAND + NEW_PAD
    ridx = _iota((rows, 1), 0)
    new_i = ridx - BAND

    kc, vc = cak_ref[0], cav_ref[0]
    kn, vn = kan_ref[0], van_ref[0]
    reps = A_Q // A_KV
    k_all = jnp.concatenate([jnp.concatenate([kc] * reps, axis=1), jnp.concatenate([kn] * reps, axis=1)], axis=0)
    v_all = jnp.concatenate([jnp.concatenate([vc] * reps, axis=1), jnp.concatenate([vn] * reps, axis=1)], axis=0)
    qa = qa_ref[0]
    ea = ea_ref[...]
    sink_row = sink_ref[...]
    out_rows = []
    for j in range(n_new):
        valid = ((ridx < BAND) & (ridx >= j)) | ((new_i >= 0) & (new_i <= j))
        o, _ = _attend_rows(qa[j:j + 1, :], k_all, v_all, valid, ea, sink_row)
        out_rows.append(o)
    out_rows.append(jnp.zeros((NEW_PAD - n_new, A_Q), F32))
    oa_ref[0] = jnp.concatenate(out_rows, axis=0)

    qb = qb_ref[0]
    kn, vn = kbn_ref[0], vbn_ref[0]
    eb = eb_ref[...]
    caches = ((ck1_ref, cv1_ref), (ck4_ref, cv4_ref), (ck16_ref, cv16_ref))
    out_rows = []
    for j in range(n_new):
        q_row = qb[j:j + 1, :]
        os_, ls_ = [], []
        for (ck_ref, cv_ref), d in zip(caches, B_DILATIONS):
            if d == 1:
                kc, vc = ck_ref[0], cv_ref[0]
                valid = ((ridx < BAND) & (ridx >= j)) | ((new_i >= 0) & (new_i <= j))
            else:
                kc, vc = ck_ref[0, :, j * B_W:(j + 1) * B_W], cv_ref[0, :, j * B_W:(j + 1) * B_W]
                valid = (ridx < BAND) | (new_i == j)
            o, l = _attend_rows(q_row, jnp.concatenate([kc, kn], axis=0), jnp.concatenate([vc, vn], axis=0),
                                valid, eb, None)
            os_.append(o)
            ls_.append(l)
        m = jnp.maximum(jnp.maximum(ls_[0], ls_[1]), ls_[2])
        ws = [jnp.exp(l - m) for l in ls_]
        out_rows.append((ws[0] * os_[0] + ws[1] * os_[1] + ws[2] * os_[2]) / (ws[0] + ws[1] + ws[2]))
    out_rows.append(jnp.zeros((NEW_PAD - n_new, B_W), F32))
    ob_ref[0] = jnp.concatenate(out_rows, axis=0)


def _seg_ones(w, dtype):
    r = jnp.arange(w) // HEAD_DIM
    return (r[:, None] == r[None, :]).astype(dtype)


def sample_attn(qa, ka, va, cache_a_k, cache_a_v, sink, qb, kb, vb, cache_b_k, cache_b_v, layer, n_new):
    nb = qa.shape[0]
    depth = cache_a_k.shape[0]
    pa = cache_a_k.shape[2]
    pb = cache_b_k.shape[2]
    assert pa == BAND and pb == BAND * B_DILATIONS[-1] and n_new <= B_DILATIONS[1]
    base = layer * nb
    cak = cache_a_k.reshape(depth * nb, pa, A_KV)
    cav = cache_a_v.reshape(depth * nb, pa, A_KV)
    new = lambda w: pl.BlockSpec((1, NEW_PAD, w), lambda b: (b, 0, 0))
    const = lambda s: pl.BlockSpec(s, lambda b: (0, 0))
    in_specs = [new(A_Q), new(A_KV), new(A_KV),
                pl.BlockSpec((1, pa, A_KV), lambda b: (base + b, 0, 0)),
                pl.BlockSpec((1, pa, A_KV), lambda b: (base + b, 0, 0)),
                const((1, A_Q)), const((A_Q, A_Q)),
                new(B_W), new(B_W), new(B_W)]
    args = [qa, ka, va, cak, cav, sink, _seg_ones(A_Q, BF16), qb, kb, vb]
    for d in B_DILATIONS:
        width = B_W if d == 1 else n_new * B_W
        blk = pb // d // BAND - 1
        for c in (cache_b_k, cache_b_v):
            args.append(c.reshape(depth * nb, pb // d, d * B_W))
            in_specs.append(pl.BlockSpec((1, BAND, width), lambda b, blk=blk: (base + b, blk, 0)))
    args.append(_seg_ones(B_W, BF16))
    in_specs.append(const((B_W, B_W)))
    return pl.pallas_call(
        functools.partial(_sample_attn_kernel, n_new=n_new),
        grid=(nb,),
        in_specs=in_specs,
        out_specs=[new(A_Q), new(B_W)],
        out_shape=[jax.ShapeDtypeStruct((nb, NEW_PAD, A_Q), F32), jax.ShapeDtypeStruct((nb, NEW_PAD, B_W), F32)],
        compiler_params=_cparams(("parallel",)),
        name="sample_attn",
    )(*args)


def _rwkv_prep_kernel(z_ref, zfirst_ref, mu_ref, w0_ref, a0_ref, kk_ref, ka_ref, w2_ref, a2_ref, g2_ref, seg_ref,
                      r_out, k_out, v_out, al_out, b_out, lw_out, g_out, carry_ref):
    tile = pl.program_id(1)
    z = z_ref[...]
    tm = z.shape[0]

    @pl.when(tile == 0)
    def _():
        carry_ref[...] = zfirst_ref[0]

    prev = jnp.where(_iota((tm, 1), 0) == 0, carry_ref[...], pltpu.roll(z, 1, axis=0))
    carry_ref[...] = z[tm - 1:tm, :]
    zs = z + (prev - z) * mu_ref[...]
    r = zs[:, 0:C_W]
    k = zs[:, C_W:2 * C_W]
    v = zs[:, 2 * C_W:3 * C_W]
    lora = zs[:, 3 * C_W:3 * C_W + LANES]
    zg = zs[:, 3 * C_W + LANES:]
    w_log = -jax.nn.softplus(-(w0_ref[...] + _dot_hi(jnp.tanh(lora), w2_ref[...]))) - 0.5
    a = jax.nn.sigmoid(a0_ref[...] + _dot_hi(lora, a2_ref[...]))
    g = _dot_hi(jax.nn.sigmoid(zg), g2_ref[...])
    kk = k * kk_ref[...]
    kk = kk / jnp.maximum(jnp.sqrt(_dot_hi(kk * kk, seg_ref[...])), 1e-12)
    r_out[...] = r
    k_out[...] = k * (1.0 + (a - 1.0) * ka_ref[...])
    v_out[...] = v
    al_out[...] = -kk
    b_out[...] = kk * a
    lw_out[...] = -jnp.exp(w_log)
    g_out[...] = g


def rwkv_prep(zc, zfirst, lw, n_seg, tm):
    t = zc.shape[0]
    tiles = t // n_seg // tm
    row = lambda b, i: (b * tiles + i, 0)
    const = lambda s: pl.BlockSpec(s, lambda b, i: (0, 0))
    vec = lambda a: a.reshape(1, -1)
    zero64 = jnp.zeros((HEAD_DIM, C_W), F32)
    w2p = jnp.concatenate([lw['c_w2'], zero64], axis=0)
    a2p = jnp.concatenate([zero64, lw['c_a2']], axis=0)
    out_spec = pl.BlockSpec((tm, C_W), row)
    return pl.pallas_call(
        _rwkv_prep_kernel,
        grid=(n_seg, tiles),
        in_specs=[pl.BlockSpec((tm, C_PROJ), row),
                  pl.BlockSpec((1, 1, C_PROJ), lambda b, i: (b, 0, 0)),
                  const((1, C_PROJ)), const((1, C_W)), const((1, C_W)), const((1, C_W)), const((1, C_W)),
                  const((LANES, C_W)), const((LANES, C_W)), const((LANES, C_W)), const((C_W, C_W))],
        out_specs=[out_spec] * 7,
        out_shape=[jax.ShapeDtypeStruct((t, C_W), F32)] * 7,
        scratch_shapes=[pltpu.VMEM((1, C_PROJ), F32)],
        compiler_params=_cparams(("parallel", "arbitrary")),
        name="rwkv_prep",
    )(zc, zfirst, vec(lw['c_mu']), vec(lw['c_w0']), vec(lw['c_a0']), vec(lw['c_k_k']), vec(lw['c_k_a']),
      w2p, a2p, lw['c_g2'], _seg_ones(C_W, F32))


def _pair_block_diag(x, mask):
    return jnp.where(mask, jnp.concatenate([x, x], axis=0), 0.0).astype(BF16)


def _rwkv_scan_kernel(r_ref, k_ref, v_ref, al_ref, b_ref, lw_ref, g_ref, s0_ref, rk_ref, lnw_ref, lnb_ref, seg_ref,
                      o_ref, sfin_ref, s_ref, *, valid_len):
    c = pl.program_id(1)
    nc = pl.num_programs(1)
    pairs = C_W // LANES
    ri = _iota((LANES, LANES), 0)
    ci = _iota((LANES, LANES), 1)
    bd_mask = (ri // HEAD_DIM) == (ci // HEAD_DIM)

    @pl.when(c == 0)
    def _():
        s_ref[...] = s0_ref[0]

    r, k, v, al, b, lw = r_ref[...], k_ref[...], v_ref[...], al_ref[...], b_ref[...], lw_ref[...]
    if valid_len < CHUNK:
        live = _iota((CHUNK, 1), 0) < valid_len
        r, k, v, al, b, lw = [jnp.where(live, x, 0.0) for x in (r, k, v, al, b, lw)]

    ti = _iota((CHUNK, CHUNK), 0)
    si = _iota((CHUNK, CHUNK), 1)
    cum = _dot_hi((si <= ti).astype(F32), lw)
    cum_end = cum[CHUNK - 1:CHUNK, :]
    e_neg = jnp.exp(-cum)
    e_rem = jnp.exp(cum_end - cum)
    a_t = al * jnp.exp(cum - lw)
    b_t = b * e_neg
    k_t = k * e_neg
    r_t = r * jnp.exp(cum)
    b_h = b * e_rem
    k_h = k * e_rem
    g_end = jnp.exp(cum_end)

    tt = _iota((CHUNK, LANES), 0)
    ss = _iota((CHUNK, LANES), 1) % CHUNK
    strict = tt > ss
    incl = tt >= ss
    eye = (tt == ss).astype(F32)
    seg = seg_ref[...]

    for m in range(pairs):
        sl = slice(m * LANES, (m + 1) * LANES)
        at, bt, kt, rt, bh, kh, vv = [x[:, sl] for x in (a_t, b_t, k_t, r_t, b_h, k_h, v)]
        bd = lambda x: _pair_block_diag(x, bd_mask)
        ar = jnp.concatenate([at, rt], axis=0)
        gb = _dot_nt(ar, bd(bt))
        gk = _dot_nt(ar, bd(kt))
        m_ab = jnp.where(strict, gb[:CHUNK], 0.0)
        n_rb = jnp.where(incl, gb[CHUNK:], 0.0)
        m_ak = jnp.where(strict, gk[:CHUNK], 0.0)
        n_rk = jnp.where(incl, gk[CHUNK:], 0.0)
        t_inv = eye + jnp.where(tt // 2 == ss // 2, m_ab, 0.0)
        blk = 4
        while blk <= CHUNK:
            joins = (tt // blk == ss // blk) & (tt // (blk // 2) != ss // (blk // 2))
            e = jnp.where(joins, m_ab, 0.0)
            t_inv = t_inv + _dot(_dot(t_inv, bd(e)), bd(t_inv))
            blk *= 2
        p = _dot(t_inv, bd(at))
        q = _dot(t_inv, bd(_dot(m_ak, bd(vv))))
        nv = _dot(n_rk, bd(vv))
        s_prev = s_ref[m]
        uy = _dot_nt(jnp.concatenate([p, rt], axis=0), s_prev)
        u = uy[:CHUNK] + q
        y = uy[CHUNK:] + _dot(n_rb, bd(u)) + nv
        upd = _dot(jnp.concatenate([u, vv], axis=0).T, jnp.concatenate([bh, kh], axis=0))
        s_ref[m] = s_prev * g_end[:, sl] + jnp.where(bd_mask, upd, 0.0)

        mu = _dot_hi(y, seg) * (1.0 / HEAD_DIM)
        dy = y - mu
        var = _dot_hi(dy * dy, seg) * (1.0 / HEAD_DIM)
        yn = dy * lax.rsqrt(var + GN_EPS) * lnw_ref[:, sl] + lnb_ref[:, sl]
        bonus = _dot_hi(r[:, sl] * k[:, sl] * rk_ref[:, sl], seg)
        o_ref[:, sl] = (yn + bonus * vv) * g_ref[:, sl]

    @pl.when(c == nc - 1)
    def _():
        sfin_ref[0] = s_ref[...]


def _state_to_pairs(s):
    n, h, hd, _ = s.shape
    s = s.reshape(n, h // 2, 2, hd, hd)
    bd = s[:, :, :, :, None, :] * jnp.eye(2, dtype=s.dtype)[None, None, :, None, :, None]
    return bd.reshape(n, h // 2, 2 * hd, 2 * hd)


def _pairs_to_state(bd):
    n, p, w, _ = bd.shape
    hd = w // 2
    bd = bd.reshape(n, p, 2, hd, 2, hd)
    return jnp.stack([bd[:, :, 0, :, 0, :], bd[:, :, 1, :, 1, :]], axis=2).reshape(n, 2 * p, hd, hd)


def rwkv_scan(prep, s0, lw, n_seq, valid_len):
    t = prep[0].shape[0]
    nc = t // n_seq // CHUNK
    pairs = C_W // LANES
    row = pl.BlockSpec((CHUNK, C_W), lambda b, c: (b * nc + c, 0))
    const = lambda s: pl.BlockSpec(s, lambda b, c: (0, 0))
    state = pl.BlockSpec((1, pairs, LANES, LANES), lambda b, c: (b, 0, 0, 0))
    vec = lambda a: a.reshape(1, -1)
    out, s_fin = pl.pallas_call(
        functools.partial(_rwkv_scan_kernel, valid_len=valid_len),
        grid=(n_seq, nc),
        in_specs=[row] * 7 + [state, const((1, C_W)), const((1, C_W)), const((1, C_W)), const((LANES, LANES))],
        out_specs=[row, state],
        out_shape=[jax.ShapeDtypeStruct((t, C_W), F32), jax.ShapeDtypeStruct((n_seq, pairs, LANES, LANES), F32)],
        scratch_shapes=[pltpu.VMEM((pairs, LANES, LANES), F32)],
        compiler_params=_cparams(("parallel", "arbitrary")),
        name="rwkv_scan",
    )(*prep, _state_to_pairs(s0), vec(lw['c_r_k']), vec(lw['c_ln_w']), vec(lw['c_ln_b']), _seg_ones(LANES, F32))
    return out, _pairs_to_state(s_fin)


def _out_proj_kernel(x_ref, oa_ref, ob_ref, oc_ref, w_ref, y_ref):
    acc = x_ref[...]
    acc += jnp.dot(oa_ref[...].astype(BF16), w_ref[0:A_Q, :], preferred_element_type=F32)
    acc += jnp.dot(ob_ref[...].astype(BF16), w_ref[A_Q:A_Q + B_W, :], preferred_element_type=F32)
    acc += jnp.dot(oc_ref[...].astype(BF16), w_ref[A_Q + B_W:, :], preferred_element_type=F32)
    y_ref[...] = acc


def out_proj(x, oa, ob, oc, w_bf16, tm):
    t, d = x.shape
    row = lambda w: pl.BlockSpec((tm, w), lambda i: (i, 0))
    return pl.pallas_call(
        _out_proj_kernel, grid=(t // tm,),
        in_specs=[row(d), row(A_Q), row(B_W), row(C_W), pl.BlockSpec(w_bf16.shape, lambda i: (0, 0))],
        out_specs=row(d), out_shape=jax.ShapeDtypeStruct((t, d), F32),
        compiler_params=_cparams(("parallel",)), name="out_proj",
    )(x, oa, ob, oc, w_bf16)


FF_CHUNK = 1024


def _mlp_kernel(x_ref, g_ref, up_ref, down_ref, y_ref):
    x = x_ref[...]
    h = (x * lax.rsqrt(jnp.mean(x * x, axis=-1, keepdims=True) + RMS_EPS) * g_ref[...]).astype(BF16)
    acc = x
    for c in range(up_ref.shape[1] // FF_CHUNK):
        u = jnp.dot(h, up_ref[:, c * FF_CHUNK:(c + 1) * FF_CHUNK], preferred_element_type=F32)
        u = jnp.square(jnp.maximum(u, 0.0)).astype(BF16)
        acc += jnp.dot(u, down_ref[c * FF_CHUNK:(c + 1) * FF_CHUNK, :], preferred_element_type=F32)
    y_ref[...] = acc


def mlp(x, g, up_bf16, down_bf16, tm):
    t, d = x.shape
    row = pl.BlockSpec((tm, d), lambda i: (i, 0))
    return pl.pallas_call(
        _mlp_kernel, grid=(t // tm,),
        in_specs=[row, pl.BlockSpec((1, d), lambda i: (0, 0)),
                  pl.BlockSpec(up_bf16.shape, lambda i: (0, 0), pipeline_mode=pl.Buffered(1)),
                  pl.BlockSpec(down_bf16.shape, lambda i: (0, 0), pipeline_mode=pl.Buffered(1))],
        out_specs=row, out_shape=jax.ShapeDtypeStruct((t, d), F32),
        compiler_params=_cparams(("parallel",)), name="mlp",
    )(x, g.reshape(1, d), up_bf16, down_bf16)


def _final_norm_kernel(x_ref, g_ref, y_ref):
    x = x_ref[...]
    y_ref[...] = x * lax.rsqrt(jnp.mean(x * x, axis=-1, keepdims=True) + RMS_EPS) * g_ref[...]


def final_norm(x, g, tm):
    t, d = x.shape
    row = pl.BlockSpec((tm, d), lambda i: (i, 0))
    return pl.pallas_call(
        _final_norm_kernel, grid=(t // tm,),
        in_specs=[row, pl.BlockSpec((1, d), lambda i: (0, 0))],
        out_specs=row, out_shape=jax.ShapeDtypeStruct((t, d), F32),
        compiler_params=_cparams(("parallel",)), name="final_norm",
    )(x, g.reshape(1, d))


PREP_ROWS = 256
A_PERM = (0, 2, 1, 3)


def _permute_a_heads(w, axis):
    take = lambda a, b: lax.slice_in_dim(w, a, b, axis=axis)
    parts = [take(h * HEAD_DIM, (h + 1) * HEAD_DIM) for h in A_PERM] + [take(A_Q, w.shape[axis])]
    return jnp.concatenate(parts, axis=axis)


def _pad_rows(x, n, l, lp):
    return jnp.pad(x.reshape(n, l, -1), ((0, 0), (0, lp - l), (0, 0)))


def _row_tile(t):
    return 512 if t % 512 == 0 else t


def trunk_layer(x, lw, n, past, layer):
    t = x.shape[0]
    l = t // n
    tm = _row_tile(t)
    qa, ka, va, qb, kb, vb, zc = norm_proj(x, lw['norm1_g'], lw['w_in'], tm)
    if past is None:
        oa, = band_attn(qa, ka, va, n=n, d=1, kv_of_q=(0, 0), sink=lw['sink'])
        outs, lses = [], []
        for d in B_DILATIONS:
            o, lse = band_attn(qb, kb, vb, n=n, d=d, kv_of_q=(0, 1, 2), emit_lse=True)
            outs.append(o)
            lses.append(lse)
        ob = merge_patterns(outs, lses, tm)
        prep = rwkv_prep(zc, jnp.zeros((n, 1, C_PROJ), F32), lw, n, min(PREP_ROWS, l))
        oc, s_fin = rwkv_scan(prep, jnp.zeros((n, C_W // HEAD_DIM, HEAD_DIM, HEAD_DIM), F32), lw, n, CHUNK)
    else:
        cache_a_k, cache_a_v, cache_b_k, cache_b_v, state_wkv, state_shift = past
        new = lambda a: _pad_rows(a, n, l, NEW_PAD)
        oa, ob = sample_attn(new(qa), new(ka), new(va), cache_a_k, cache_a_v, lw['sink'],
                             new(qb), new(kb), new(vb), cache_b_k, cache_b_v, layer, l)
        oa = oa[:, :l].reshape(t, A_Q)
        ob = ob[:, :l].reshape(t, B_W)
        z_before = state_shift[layer]
        zp = _pad_rows(zc, n, l, CHUNK)
        zp = zp.at[:-1, CHUNK - 1].set(z_before[1:])
        prep = rwkv_prep(zp.reshape(n * CHUNK, C_PROJ), z_before[0].reshape(1, 1, C_PROJ), lw, 1,
                         min(PREP_ROWS, n * CHUNK))
        oc, s_fin = rwkv_scan(prep, state_wkv[layer], lw, n, l)
        oc = oc.reshape(n, CHUNK, C_W)[:, :l].reshape(t, C_W)
    x = out_proj(x, oa, ob, oc, lw['w_out'], tm)
    x = mlp(x, lw['norm2_g'], lw['mlp_up'], lw['mlp_down'], tm)
    heads = lambda a, h: a.reshape(n, l, h, HEAD_DIM)
    ra, rb = min(BAND, l), min(BAND * B_DILATIONS[-1], l)
    new_state = (heads(ka, 2)[:, l - ra:], heads(va, 2)[:, l - ra:], heads(kb, 6)[:, l - rb:], heads(vb, 6)[:, l - rb:],
                 s_fin, zc.reshape(n, l, C_PROJ)[:, -1])
    return x, new_state


def kernel(x_prompt, x_sample, cache_a_k, cache_a_v, cache_b_k, cache_b_v, state_c_wkv, state_c_shift, norm1_g, norm2_g, w_in, w_out, a_sink, c_mu, c_w0, c_w2, c_a0, c_a2, c_g2, c_k_k, c_k_a, c_r_k, c_ln_w, c_ln_b, mlp_up, mlp_down, final_norm_g):
    depth = w_in.shape[0]
    n_p, l_p, d_model = x_prompt.shape
    n_s, l_s, _ = x_sample.shape
    w_in_b = _permute_a_heads(w_in, 2).astype(BF16)
    w_out_b = _permute_a_heads(w_out, 1).astype(BF16)
    up_b = mlp_up.astype(BF16)
    down_b = mlp_down.astype(BF16)
    sink = jnp.repeat(a_sink[:, jnp.array(A_PERM)], HEAD_DIM, axis=1).reshape(depth, 1, A_Q)
    past = (cache_a_k, cache_a_v, cache_b_k, cache_b_v, state_c_wkv, state_c_shift)

    yp = x_prompt.reshape(n_p * l_p, d_model)
    ys = x_sample.reshape(n_s * l_s, d_model)
    p_new, s_new = [], []
    for l in range(depth):
        lw = {'norm1_g': norm1_g[l], 'norm2_g': norm2_g[l], 'w_in': w_in_b[l], 'w_out': w_out_b[l], 'sink': sink[l],
              'c_mu': c_mu[l], 'c_w0': c_w0[l], 'c_w2': c_w2[l], 'c_a0': c_a0[l], 'c_a2': c_a2[l], 'c_g2': c_g2[l],
              'c_k_k': c_k_k[l], 'c_k_a': c_k_a[l], 'c_r_k': c_r_k[l], 'c_ln_w': c_ln_w[l], 'c_ln_b': c_ln_b[l],
              'mlp_up': up_b[l], 'mlp_down': down_b[l]}
        yp, st_p = trunk_layer(yp, lw, n_p, None, l)
        ys, st_s = trunk_layer(ys, lw, n_s, past, l)
        p_new.append(st_p)
        s_new.append(st_s)
    p_state = [jnp.stack(t) for t in zip(*p_new)]
    s_state = [jnp.stack(t) for t in zip(*s_new)]
    y_prompt = final_norm(yp, final_norm_g, _row_tile(yp.shape[0])).reshape(n_p, l_p, d_model)
    y_sample = final_norm(ys, final_norm_g, _row_tile(ys.shape[0])).reshape(n_s, l_s, d_model)
    return (y_prompt, y_sample, *p_state, *s_state)
```

```python
import functools

import jax
import jax.numpy as jnp
from jax import lax
from jax.experimental import pallas as pl
from jax.experimental.pallas import tpu as pltpu

F32 = jnp.float32
BF16 = jnp.bfloat16

LANES = 128
HEAD_DIM = 64
BAND = 128
CHUNK = 64
RMS_EPS = 1e-5
GN_EPS = 64e-5
NEG = -1e30
VMEM_LIMIT = 56 * 1024 * 1024

A_Q, A_KV, B_W, C_W = 256, 128, 384, 384
C_PROJ = 1408
SPLITS = (0, 256, 384, 512, 896, 1280, 1664, 3072)
B_DILATIONS = (1, 4, 16)


def _cparams(sem):
    return pltpu.CompilerParams(dimension_semantics=sem, vmem_limit_bytes=VMEM_LIMIT)


def _dot(a, b):
    return jnp.dot(a.astype(BF16), b.astype(BF16), preferred_element_type=F32)


def _dot_nt(a, b):
    return lax.dot_general(a.astype(BF16), b.astype(BF16), (((1,), (1,)), ((), ())),
                           preferred_element_type=F32)


def _split(a):
    hi = a.astype(BF16)
    return hi, (a - hi.astype(F32)).astype(BF16)


def _dot_x2(a, b_exact):
    hi, lo = _split(a)
    b = b_exact.astype(BF16)
    return jnp.dot(hi, b, preferred_element_type=F32) + jnp.dot(lo, b, preferred_element_type=F32)


def _dot_x3(a, b):
    ah, al = _split(a)
    bh, bl = _split(b)
    return (jnp.dot(ah, bh, preferred_element_type=F32) + jnp.dot(al, bh, preferred_element_type=F32)
            + jnp.dot(ah, bl, preferred_element_type=F32))


def _iota(shape, dim):
    return lax.broadcasted_iota(jnp.int32, shape, dim)


def _norm_proj_kernel(x_ref, g_ref, w_ref, *out_refs):
    x = x_ref[...]
    h = x * lax.rsqrt(jnp.mean(x * x, axis=-1, keepdims=True) + RMS_EPS) * g_ref[...]
    hb = h.astype(BF16)
    for o_ref, a, b in zip(out_refs, SPLITS[:-1], SPLITS[1:]):
        o_ref[...] = jnp.dot(hb, w_ref[:, a:b], preferred_element_type=F32)


def norm_proj(x, g, w_bf16, tm):
    t, d = x.shape
    widths = [b - a for a, b in zip(SPLITS[:-1], SPLITS[1:])]
    return pl.pallas_call(
        _norm_proj_kernel,
        grid=(t // tm,),
        in_specs=[pl.BlockSpec((tm, d), lambda i: (i, 0)),
                  pl.BlockSpec((1, d), lambda i: (0, 0)),
                  pl.BlockSpec(w_bf16.shape, lambda i: (0, 0))],
        out_specs=[pl.BlockSpec((tm, w), lambda i: (i, 0)) for w in widths],
        out_shape=[jax.ShapeDtypeStruct((t, w), F32) for w in widths],
        compiler_params=_cparams(("parallel",)),
        name="norm_proj",
    )(x, g.reshape(1, d), w_bf16)


def _band_attn_kernel(*refs, nq, kv_of_q, has_sink, emit_lse):
    q_ref, kp_ref, kc_ref, vp_ref, vc_ref = refs[:5]
    pos = 5
    sink_ref = None
    if has_sink:
        sink_ref = refs[pos]
        pos += 1
    o_ref = refs[pos]
    lse_ref = refs[pos + 1] if emit_lse else None

    i = pl.program_id(2)
    qi = _iota((BAND, 2 * BAND), 0)
    kj = _iota((BAND, 2 * BAND), 1)
    mask = (kj >= qi) & (kj <= qi + BAND) & ((kj >= BAND) | (i > 0))
    lane = _iota((BAND, LANES), 1)
    lo = lane < HEAD_DIM
    scale = HEAD_DIM ** -0.5
    for t in range(nq):
        qs = slice(t * LANES, (t + 1) * LANES)
        ks = slice(kv_of_q[t] * LANES, (kv_of_q[t] + 1) * LANES)
        q = q_ref[:, qs]
        k = jnp.concatenate([kp_ref[:, ks], kc_ref[:, ks]], axis=0).astype(BF16)
        v = jnp.concatenate([vp_ref[:, ks], vc_ref[:, ks]], axis=0).astype(BF16)
        outs, lses = [], []
        for half in range(2):
            sel = lo if half == 0 else jnp.logical_not(lo)
            qm = jnp.where(sel, q, 0.0)
            s = _dot_nt(qm, k) * scale
            s = jnp.where(mask, s, NEG)
            m = jnp.max(s, axis=-1, keepdims=True)
            if has_sink:
                c = t * LANES + half * HEAD_DIM
                sk = sink_ref[0:1, c:c + 1]
                m = jnp.maximum(m, sk)
            p = jnp.exp(s - m)
            den = jnp.sum(p, axis=-1, keepdims=True)
            if has_sink:
                den = den + jnp.exp(sk - m)
            outs.append(_dot(p, v) / den)
            lses.append(m + jnp.log(den))
        o_ref[:, qs] = jnp.where(lo, outs[0], outs[1])
        if emit_lse:
            lse_ref[:, qs] = jnp.where(lo, lses[0], lses[1])


def band_attn(q, k, v, *, n, d, kv_of_q, sink=None, emit_lse=False):
    t, wq = q.shape
    wkv = k.shape[1]
    nq = wq // LANES
    nb = t // n // d // BAND
    qv = q.reshape(t // d, d * wq)
    kv = k.reshape(t // d, d * wkv)
    vv = v.reshape(t // d, d * wkv)
    cur = lambda b, r, i: (b * nb + i, r)
    prev = lambda b, r, i: (b * nb + jnp.maximum(i - 1, 0), r)
    in_specs = [pl.BlockSpec((BAND, wq), cur),
                pl.BlockSpec((BAND, wkv), prev), pl.BlockSpec((BAND, wkv), cur),
                pl.BlockSpec((BAND, wkv), prev), pl.BlockSpec((BAND, wkv), cur)]
    args = [qv, kv, kv, vv, vv]
    if sink is not None:
        in_specs.append(pl.BlockSpec((1, wq), lambda b, r, i: (0, 0)))
        args.append(sink)
    n_out = 2 if emit_lse else 1
    outs = pl.pallas_call(
        functools.partial(_band_attn_kernel, nq=nq, kv_of_q=kv_of_q, has_sink=sink is not None,
                          emit_lse=emit_lse),
        grid=(n, d, nb),
        in_specs=in_specs,
        out_specs=[pl.BlockSpec((BAND, wq), cur)] * n_out,
        out_shape=[jax.ShapeDtypeStruct((t // d, d * wq), F32)] * n_out,
        compiler_params=_cparams(("parallel", "parallel", "arbitrary")),
        name="band_attn_d%d" % d,
    )(*args)
    return [o.reshape(t, wq) for o in outs]


def _merge_kernel(o1, o2, o3, l1, l2, l3, out_ref):
    a, b, c = l1[...], l2[...], l3[...]
    m = jnp.maximum(jnp.maximum(a, b), c)
    wa, wb, wc = jnp.exp(a - m), jnp.exp(b - m), jnp.exp(c - m)
    out_ref[...] = (wa * o1[...] + wb * o2[...] + wc * o3[...]) / (wa + wb + wc)


def merge_patterns(outs, lses, tm):
    t, w = outs[0].shape
    spec = pl.BlockSpec((tm, w), lambda i: (i, 0))
    return pl.pallas_call(
        _merge_kernel, grid=(t // tm,), in_specs=[spec] * 6, out_specs=spec,
        out_shape=jax.ShapeDtypeStruct((t, w), F32),
        compiler_params=_cparams(("parallel",)), name="merge_patterns",
    )(*outs, *lses)


NEW_PAD = 8


def _attend_t(q, kt_c, vt_c, kt_n, vt_n, valid_c, valid_n, sink):
    scale = HEAD_DIM ** -0.5
    qb = q.astype(BF16)
    s_c = jnp.where(valid_c, _dot(qb, kt_c) * scale, NEG)
    s_n = jnp.where(valid_n, _dot(qb, kt_n) * scale, NEG)
    m = jnp.maximum(jnp.max(s_c, axis=-1, keepdims=True), jnp.max(s_n, axis=-1, keepdims=True))
    if sink is not None:
        m = jnp.maximum(m, sink)
    p_c = jnp.exp(s_c - m)
    p_n = jnp.exp(s_n - m)
    den = jnp.sum(p_c, axis=-1, keepdims=True) + jnp.sum(p_n, axis=-1, keepdims=True)
    if sink is not None:
        den = den + jnp.exp(sink - m)
    o = (_dot_nt(p_c, vt_c) + _dot_nt(p_n, vt_n)) / den
    return o, m + jnp.log(den)


def _sample_attn_kernel(qa_ref, kan_ref, van_ref, cak_ref, cav_ref, sink_ref,
                        qb_ref, kbn_ref, vbn_ref, cbk_ref, cbv_ref, oa_ref, ob_ref, *, n_new):
    pa = cak_ref.shape[-1]
    pb = cbk_ref.shape[-1]
    new_i = _iota((NEW_PAD, LANES), 1)
    live_n = new_i < n_new

    j = _iota((NEW_PAD, pa), 0) % n_new
    valid_c = _iota((NEW_PAD, pa), 1) >= j
    valid_n = live_n & (new_i <= _iota((NEW_PAD, LANES), 0) % n_new)
    for g in range(cak_ref.shape[2]):
        o, _ = _attend_t(qa_ref[0, g], cak_ref[0, 0, g].astype(BF16), cav_ref[0, 0, g].astype(BF16),
                         kan_ref[0, g], van_ref[0, g], valid_c, valid_n, sink_ref[g][:, 0:1])
        oa_ref[0, g] = o

    jn = _iota((NEW_PAD, LANES), 0)
    for h in range(cbk_ref.shape[2]):
        q = qb_ref[0, h]
        outs, lses = [], []
        for d in B_DILATIONS:
            span = BAND * d
            kt = cbk_ref[0, 0, h, :, pb - span:].astype(BF16)
            vt = cbv_ref[0, 0, h, :, pb - span:].astype(BF16)
            jc = _iota((NEW_PAD, span), 0)
            pc = _iota((NEW_PAD, span), 1)
            if d == 1:
                valid_c = pc >= jc
                valid_n = live_n & (new_i <= jn)
            else:
                valid_c = pc % d == jc
                valid_n = live_n & (new_i == jn)
            o, lse = _attend_t(q, kt, vt, kbn_ref[0, h], vbn_ref[0, h], valid_c, valid_n, None)
            outs.append(o)
            lses.append(lse)
        m = jnp.maximum(jnp.maximum(lses[0], lses[1]), lses[2])
        ws = [jnp.exp(l - m) for l in lses]
        ob_ref[0, h] = (ws[0] * outs[0] + ws[1] * outs[1] + ws[2] * outs[2]) / (ws[0] + ws[1] + ws[2])


def _seg_ones(w, dtype):
    r = jnp.arange(w) // HEAD_DIM
    return (r[:, None] == r[None, :]).astype(dtype)


def _new_keys_t(x, n, l, heads):
    xt = x.reshape(n, l, heads, HEAD_DIM).transpose(0, 2, 3, 1)
    return jnp.pad(xt, ((0, 0), (0, 0), (0, 0), (0, LANES - l)))


def sample_attn(qa, ka, va, cache_a_k_t, cache_a_v_t, sink, qb, kb, vb, cache_b_k_t, cache_b_v_t, layer, n, l):
    ga, gb = cache_a_k_t.shape[2], cache_b_k_t.shape[2]
    pa, pb = cache_a_k_t.shape[4], cache_b_k_t.shape[4]
    reps = A_Q // A_KV
    assert pa == BAND and pb == BAND * B_DILATIONS[-1] and l * reps == NEW_PAD and l <= B_DILATIONS[1]
    qa_g = qa.reshape(n, l, reps, ga, HEAD_DIM).transpose(0, 3, 2, 1, 4).reshape(n, ga, NEW_PAD, HEAD_DIM)
    sink_g = jnp.broadcast_to(sink.reshape(ga, reps, 1, 1), (ga, reps, l, LANES)).reshape(ga, NEW_PAD, LANES)
    qb_g = jnp.pad(qb.reshape(n, l, gb, HEAD_DIM).transpose(0, 2, 1, 3), ((0, 0), (0, 0), (0, NEW_PAD - l), (0, 0)))
    rows = lambda g: pl.BlockSpec((1, g, NEW_PAD, HEAD_DIM), lambda b: (b, 0, 0, 0))
    cols = lambda g: pl.BlockSpec((1, g, HEAD_DIM, LANES), lambda b: (b, 0, 0, 0))
    cache = lambda g, p: pl.BlockSpec((1, 1, g, HEAD_DIM, p), lambda b: (layer, b, 0, 0, 0))
    oa, ob = pl.pallas_call(
        functools.partial(_sample_attn_kernel, n_new=l),
        grid=(n,),
        in_specs=[rows(ga), cols(ga), cols(ga), cache(ga, pa), cache(ga, pa),
                  pl.BlockSpec((ga, NEW_PAD, LANES), lambda b: (0, 0, 0)),
                  rows(gb), cols(gb), cols(gb), cache(gb, pb), cache(gb, pb)],
        out_specs=[rows(ga), rows(gb)],
        out_shape=[jax.ShapeDtypeStruct((n, ga, NEW_PAD, HEAD_DIM), F32),
                   jax.ShapeDtypeStruct((n, gb, NEW_PAD, HEAD_DIM), F32)],
        compiler_params=_cparams(("parallel",)),
        name="sample_attn",
    )(qa_g, _new_keys_t(ka, n, l, ga), _new_keys_t(va, n, l, ga), cache_a_k_t, cache_a_v_t, sink_g,
      qb_g, _new_keys_t(kb, n, l, gb), _new_keys_t(vb, n, l, gb), cache_b_k_t, cache_b_v_t)
    oa = oa.reshape(n, ga, reps, l, HEAD_DIM).transpose(0, 3, 2, 1, 4).reshape(n * l, A_Q)
    ob = ob[:, :, :l].transpose(0, 2, 1, 3).reshape(n * l, B_W)
    return oa, ob


def _rwkv_prep_kernel(z_ref, zfirst_ref, mu_ref, w0_ref, a0_ref, kk_ref, ka_ref, w2_ref, a2_ref, g2_ref, seg_ref,
                      r_out, k_out, v_out, al_out, b_out, lw_out, g_out, carry_ref):
    tile = pl.program_id(1)
    z = z_ref[...]
    tm = z.shape[0]

    @pl.when(tile == 0)
    def _():
        carry_ref[...] = zfirst_ref[0]

    prev = jnp.where(_iota((tm, 1), 0) == 0, carry_ref[...], pltpu.roll(z, 1, axis=0))
    carry_ref[...] = z[tm - 1:tm, :]
    zs = z + (prev - z) * mu_ref[...]
    r = zs[:, 0:C_W]
    k = zs[:, C_W:2 * C_W]
    v = zs[:, 2 * C_W:3 * C_W]
    lora = zs[:, 3 * C_W:3 * C_W + LANES]
    zg = zs[:, 3 * C_W + LANES:]
    w_log = -jax.nn.softplus(-(w0_ref[...] + _dot_x3(jnp.tanh(lora), w2_ref[...]))) - 0.5
    a = jax.nn.sigmoid(a0_ref[...] + _dot_x3(lora, a2_ref[...]))
    g = _dot_x3(jax.nn.sigmoid(zg), g2_ref[...])
    kk = k * kk_ref[...]
    kk = kk / jnp.maximum(jnp.sqrt(_dot_x2(kk * kk, seg_ref[...])), 1e-12)
    r_out[...] = r
    k_out[...] = k * (1.0 + (a - 1.0) * ka_ref[...])
    v_out[...] = v
    al_out[...] = -kk
    b_out[...] = kk * a
    lw_out[...] = -jnp.exp(w_log)
    g_out[...] = g


def rwkv_prep(zc, zfirst, lw, n_seg, tm):
    t = zc.shape[0]
    tiles = t // n_seg // tm
    row = lambda b, i: (b * tiles + i, 0)
    const = lambda s: pl.BlockSpec(s, lambda b, i: (0, 0))
    vec = lambda a: a.reshape(1, -1)
    zero64 = jnp.zeros((HEAD_DIM, C_W), F32)
    w2p = jnp.concatenate([lw['c_w2'], zero64], axis=0)
    a2p = jnp.concatenate([zero64, lw['c_a2']], axis=0)
    out_spec = pl.BlockSpec((tm, C_W), row)
    return pl.pallas_call(
        _rwkv_prep_kernel,
        grid=(n_seg, tiles),
        in_specs=[pl.BlockSpec((tm, C_PROJ), row),
                  pl.BlockSpec((1, 1, C_PROJ), lambda b, i: (b, 0, 0)),
                  const((1, C_PROJ)), const((1, C_W)), const((1, C_W)), const((1, C_W)), const((1, C_W)),
                  const((LANES, C_W)), const((LANES, C_W)), const((LANES, C_W)), const((C_W, C_W))],
        out_specs=[out_spec] * 7,
        out_shape=[jax.ShapeDtypeStruct((t, C_W), F32)] * 7,
        scratch_shapes=[pltpu.VMEM((1, C_PROJ), F32)],
        compiler_params=_cparams(("parallel", "arbitrary")),
        name="rwkv_prep",
    )(zc, zfirst, vec(lw['c_mu']), vec(lw['c_w0']), vec(lw['c_a0']), vec(lw['c_k_k']), vec(lw['c_k_a']),
      w2p, a2p, lw['c_g2'], _seg_ones(C_W, F32))


SCAN_GROUP = 4


def _pair_block_diag(x, mask_b):
    xb = x.astype(BF16)
    return jnp.concatenate([xb, xb], axis=0) * mask_b


def _rwkv_scan_kernel(r_ref, k_ref, v_ref, al_ref, b_ref, lw_ref, g_ref, s0_ref, rk_ref, lnw_ref, lnb_ref, seg_ref,
                      o_ref, sfin_ref, s_ref, *, valid_len, group, chain):
    step = pl.program_id(1)
    pairs = C_W // LANES
    rows = group * CHUNK
    ri = _iota((LANES, LANES), 0)
    ci = _iota((LANES, LANES), 1)
    bd_mask = (ri // HEAD_DIM) == (ci // HEAD_DIM)
    mask_b = jnp.where(bd_mask, 1.0, 0.0).astype(BF16)
    bd = lambda x: _pair_block_diag(x, mask_b)

    if chain:
        @pl.when(step == 0)
        def _():
            s_ref[...] = s0_ref[0]

    r, k, v, al, b, lw = r_ref[...], k_ref[...], v_ref[...], al_ref[...], b_ref[...], lw_ref[...]
    if valid_len < CHUNK:
        live = _iota((rows, 1), 0) % CHUNK < valid_len
        r, k, v, al, b, lw = [jnp.where(live, x, 0.0) for x in (r, k, v, al, b, lw)]

    ti = _iota((rows, rows), 0)
    si = _iota((rows, rows), 1)
    tri = jnp.where((si <= ti) & (ti // CHUNK == si // CHUNK), 1.0, 0.0).astype(BF16)
    lw_hi, lw_lo = _split(lw)
    cum = jnp.dot(tri, lw_hi, preferred_element_type=F32) + jnp.dot(tri, lw_lo, preferred_element_type=F32)
    ends = [cum[(c + 1) * CHUNK - 1:(c + 1) * CHUNK, :] for c in range(group)]
    cum_end = jnp.concatenate([jnp.broadcast_to(e, (CHUNK, C_W)) for e in ends], axis=0)
    e_neg = jnp.exp(-cum)
    e_rem = jnp.exp(cum_end - cum)
    a_t = al * jnp.exp(cum - lw)
    b_t = b * e_neg
    k_t = k * e_neg
    r_t = r * jnp.exp(cum)
    b_h = b * e_rem
    k_h = k * e_rem

    tt = _iota((CHUNK, LANES), 0)
    ss = _iota((CHUNK, LANES), 1) % CHUNK
    strict = tt > ss
    incl = tt >= ss
    eye = (tt == ss).astype(F32)
    items = [(c, m) for c in range(group) for m in range(pairs)]
    tile = lambda x, it: x[it[0] * CHUNK:(it[0] + 1) * CHUNK, it[1] * LANES:(it[1] + 1) * LANES]

    gram = {}
    for it in items:
        ar = jnp.concatenate([tile(a_t, it), tile(r_t, it)], axis=0)
        bk = jnp.concatenate([bd(tile(b_t, it)), bd(tile(k_t, it))], axis=0)
        gram[it] = _dot_nt(ar, bk)
    m_ab = {it: jnp.where(strict, gram[it][:CHUNK, :LANES], 0.0) for it in items}
    n_rb = {it: jnp.where(incl, gram[it][CHUNK:, :LANES], 0.0) for it in items}
    m_ak = {it: jnp.where(strict, gram[it][:CHUNK, LANES:], 0.0) for it in items}
    n_rk = {it: jnp.where(incl, gram[it][CHUNK:, LANES:], 0.0) for it in items}
    t_inv = {it: eye + jnp.where(tt // 2 == ss // 2, m_ab[it], 0.0) for it in items}
    blk = 4
    while blk <= CHUNK:
        joins = (tt // blk == ss // blk) & (tt // (blk // 2) != ss // (blk // 2))
        de = {it: _dot(t_inv[it], bd(jnp.where(joins, m_ab[it], 0.0))) for it in items}
        t_inv = {it: t_inv[it] + _dot(de[it], bd(t_inv[it])) for it in items}
        blk *= 2
    wv = {it: _dot(jnp.concatenate([m_ak[it], n_rk[it]], axis=0), bd(tile(v, it))) for it in items}
    pq = {it: _dot(t_inv[it], jnp.concatenate([bd(tile(a_t, it)), bd(wv[it][:CHUNK])], axis=1)) for it in items}

    state = [s_ref[m] for m in range(pairs)] if chain else None
    y = {}
    for c in range(group):
        for m in range(pairs):
            it = (c, m)
            s_prev = state[m] if chain else s0_ref[c, m]
            uy = _dot_nt(jnp.concatenate([pq[it][:, :LANES], tile(r_t, it)], axis=0), s_prev)
            u = uy[:CHUNK] + pq[it][:, LANES:]
            y[it] = uy[CHUNK:] + _dot(n_rb[it], bd(u)) + wv[it][CHUNK:]
            upd = _dot(jnp.concatenate([u, tile(v, it)], axis=0).T,
                       jnp.concatenate([tile(b_h, it), tile(k_h, it)], axis=0))
            g_end = jnp.exp(ends[c][:, m * LANES:(m + 1) * LANES])
            s_new = s_prev * g_end + jnp.where(bd_mask, upd, 0.0)
            if chain:
                state[m] = s_new
            else:
                sfin_ref[c, m] = s_new
    if chain:
        for m in range(pairs):
            s_ref[m] = state[m]

    seg = seg_ref[...]
    for m in range(pairs):
        sl = slice(m * LANES, (m + 1) * LANES)
        ym = jnp.concatenate([y[(c, m)] for c in range(group)], axis=0)
        mu = _dot_x2(ym, seg) * (1.0 / HEAD_DIM)
        dy = ym - mu
        var = _dot_x2(dy * dy, seg) * (1.0 / HEAD_DIM)
        yn = dy * lax.rsqrt(var + GN_EPS) * lnw_ref[:, sl] + lnb_ref[:, sl]
        bonus = _dot_x2(r[:, sl] * k[:, sl] * rk_ref[:, sl], seg)
        o_ref[:, sl] = (yn + bonus * v[:, sl]) * g_ref[:, sl]

    if chain:
        @pl.when(step == pl.num_programs(1) - 1)
        def _():
            sfin_ref[0] = s_ref[...]


def _state_to_pairs(s):
    n, h, hd, _ = s.shape
    s = s.reshape(n, h // 2, 2, hd, hd)
    bd = s[:, :, :, :, None, :] * jnp.eye(2, dtype=s.dtype)[None, None, :, None, :, None]
    return bd.reshape(n, h // 2, 2 * hd, 2 * hd)


def _pairs_to_state(bd):
    n, p, w, _ = bd.shape
    hd = w // 2
    bd = bd.reshape(n, p, 2, hd, 2, hd)
    return jnp.stack([bd[:, :, 0, :, 0, :], bd[:, :, 1, :, 1, :]], axis=2).reshape(n, 2 * p, hd, hd)


def rwkv_scan(prep, s0, lw, n_seq, valid_len):
    t = prep[0].shape[0]
    nc = t // n_seq // CHUNK
    pairs = C_W // LANES
    chain = nc > 1
    group = min(SCAN_GROUP, nc if chain else n_seq)
    grid = (n_seq, nc // group) if chain else (n_seq // group, 1)
    steps = grid[1]
    row = pl.BlockSpec((group * CHUNK, C_W), lambda b, c: (b * steps + c, 0))
    const = lambda s: pl.BlockSpec(s, lambda b, c: (0, 0))
    state = pl.BlockSpec((1 if chain else group, pairs, LANES, LANES), lambda b, c: (b, 0, 0, 0))
    vec = lambda a: a.reshape(1, -1)
    out, s_fin = pl.pallas_call(
        functools.partial(_rwkv_scan_kernel, valid_len=valid_len, group=group, chain=chain),
        grid=grid,
        in_specs=[row] * 7 + [state, const((1, C_W)), const((1, C_W)), const((1, C_W)), const((LANES, LANES))],
        out_specs=[row, state],
        out_shape=[jax.ShapeDtypeStruct((t, C_W), F32), jax.ShapeDtypeStruct((n_seq, pairs, LANES, LANES), F32)],
        scratch_shapes=[pltpu.VMEM((pairs, LANES, LANES), F32)],
        compiler_params=_cparams(("parallel", "arbitrary")),
        name="rwkv_scan",
    )(*prep, _state_to_pairs(s0), vec(lw['c_r_k']), vec(lw['c_ln_w']), vec(lw['c_ln_b']), _seg_ones(LANES, F32))
    return out, _pairs_to_state(s_fin)


def _out_proj_kernel(x_ref, oa_ref, ob_ref, oc_ref, w_ref, y_ref):
    acc = x_ref[...]
    acc += jnp.dot(oa_ref[...].astype(BF16), w_ref[0:A_Q, :], preferred_element_type=F32)
    acc += jnp.dot(ob_ref[...].astype(BF16), w_ref[A_Q:A_Q + B_W, :], preferred_element_type=F32)
    acc += jnp.dot(oc_ref[...].astype(BF16), w_ref[A_Q + B_W:, :], preferred_element_type=F32)
    y_ref[...] = acc


def out_proj(x, oa, ob, oc, w_bf16, tm):
    t, d = x.shape
    row = lambda w: pl.BlockSpec((tm, w), lambda i: (i, 0))
    return pl.pallas_call(
        _out_proj_kernel, grid=(t // tm,),
        in_specs=[row(d), row(A_Q), row(B_W), row(C_W), pl.BlockSpec(w_bf16.shape, lambda i: (0, 0))],
        out_specs=row(d), out_shape=jax.ShapeDtypeStruct((t, d), F32),
        compiler_params=_cparams(("parallel",)), name="out_proj",
    )(x, oa, ob, oc, w_bf16)


FF_CHUNK = 1024


def _mlp_kernel(x_ref, g_ref, up_ref, down_ref, y_ref):
    x = x_ref[...]
    h = (x * lax.rsqrt(jnp.mean(x * x, axis=-1, keepdims=True) + RMS_EPS) * g_ref[...]).astype(BF16)
    acc = x
    for c in range(up_ref.shape[1] // FF_CHUNK):
        u = jnp.dot(h, up_ref[:, c * FF_CHUNK:(c + 1) * FF_CHUNK], preferred_element_type=F32)
        u = jnp.square(jnp.maximum(u, 0.0)).astype(BF16)
        acc += jnp.dot(u, down_ref[c * FF_CHUNK:(c + 1) * FF_CHUNK, :], preferred_element_type=F32)
    y_ref[...] = acc


def mlp(x, g, up_bf16, down_bf16, tm):
    t, d = x.shape
    row = pl.BlockSpec((tm, d), lambda i: (i, 0))
    return pl.pallas_call(
        _mlp_kernel, grid=(t // tm,),
        in_specs=[row, pl.BlockSpec((1, d), lambda i: (0, 0)),
                  pl.BlockSpec(up_bf16.shape, lambda i: (0, 0), pipeline_mode=pl.Buffered(1)),
                  pl.BlockSpec(down_bf16.shape, lambda i: (0, 0), pipeline_mode=pl.Buffered(1))],
        out_specs=row, out_shape=jax.ShapeDtypeStruct((t, d), F32),
        compiler_params=_cparams(("parallel",)), name="mlp",
    )(x, g.reshape(1, d), up_bf16, down_bf16)


def _final_norm_kernel(x_ref, g_ref, y_ref):
    x = x_ref[...]
    y_ref[...] = x * lax.rsqrt(jnp.mean(x * x, axis=-1, keepdims=True) + RMS_EPS) * g_ref[...]


def final_norm(x, g, tm):
    t, d = x.shape
    row = pl.BlockSpec((tm, d), lambda i: (i, 0))
    return pl.pallas_call(
        _final_norm_kernel, grid=(t // tm,),
        in_specs=[row, pl.BlockSpec((1, d), lambda i: (0, 0))],
        out_specs=row, out_shape=jax.ShapeDtypeStruct((t, d), F32),
        compiler_params=_cparams(("parallel",)), name="final_norm",
    )(x, g.reshape(1, d))


PREP_ROWS = 256
A_PERM = (0, 2, 1, 3)


def _permute_a_heads(w, axis):
    take = lambda a, b: lax.slice_in_dim(w, a, b, axis=axis)
    parts = [take(h * HEAD_DIM, (h + 1) * HEAD_DIM) for h in A_PERM] + [take(A_Q, w.shape[axis])]
    return jnp.concatenate(parts, axis=axis)


def _pad_rows(x, n, l, lp):
    return jnp.pad(x.reshape(n, l, -1), ((0, 0), (0, lp - l), (0, 0)))


def _row_tile(t):
    return 512 if t % 512 == 0 else t


def trunk_layer(x, lw, n, past, layer):
    t = x.shape[0]
    l = t // n
    tm = _row_tile(t)
    qa, ka, va, qb, kb, vb, zc = norm_proj(x, lw['norm1_g'], lw['w_in'], tm)
    if past is None:
        oa, = band_attn(qa, ka, va, n=n, d=1, kv_of_q=(0, 0), sink=lw['sink'])
        outs, lses = [], []
        for d in B_DILATIONS:
            o, lse = band_attn(qb, kb, vb, n=n, d=d, kv_of_q=(0, 1, 2), emit_lse=True)
            outs.append(o)
            lses.append(lse)
        ob = merge_patterns(outs, lses, tm)
        prep = rwkv_prep(zc, jnp.zeros((n, 1, C_PROJ), F32), lw, n, min(PREP_ROWS, l))
        oc, s_fin = rwkv_scan(prep, jnp.zeros((n, C_W // HEAD_DIM, HEAD_DIM, HEAD_DIM), F32), lw, n, CHUNK)
    else:
        cache_a_k_t, cache_a_v_t, cache_b_k_t, cache_b_v_t, state_wkv, state_shift = past
        oa, ob = sample_attn(qa, ka, va, cache_a_k_t, cache_a_v_t, lw['a_sink'],
                             qb, kb, vb, cache_b_k_t, cache_b_v_t, layer, n, l)
        z_before = state_shift[layer]
        zp = _pad_rows(zc, n, l, CHUNK)
        zp = zp.at[:-1, CHUNK - 1].set(z_before[1:])
        prep = rwkv_prep(zp.reshape(n * CHUNK, C_PROJ), z_before[0].reshape(1, 1, C_PROJ), lw, 1,
                         min(PREP_ROWS, n * CHUNK))
        oc, s_fin = rwkv_scan(prep, state_wkv[layer], lw, n, l)
        oc = oc.reshape(n, CHUNK, C_W)[:, :l].reshape(t, C_W)
    x = out_proj(x, oa, ob, oc, lw['w_out'], tm)
    x = mlp(x, lw['norm2_g'], lw['mlp_up'], lw['mlp_down'], tm)
    heads = lambda a, h: a.reshape(n, l, h, HEAD_DIM)
    ra, rb = min(BAND, l), min(BAND * B_DILATIONS[-1], l)
    new_state = (heads(ka, 2)[:, l - ra:], heads(va, 2)[:, l - ra:], heads(kb, 6)[:, l - rb:], heads(vb, 6)[:, l - rb:],
                 s_fin, zc.reshape(n, l, C_PROJ)[:, -1])
    return x, new_state


def kernel(x_prompt, x_sample, cache_a_k, cache_a_v, cache_b_k, cache_b_v, state_c_wkv, state_c_shift, norm1_g, norm2_g, w_in, w_out, a_sink, c_mu, c_w0, c_w2, c_a0, c_a2, c_g2, c_k_k, c_k_a, c_r_k, c_ln_w, c_ln_b, mlp_up, mlp_down, final_norm_g):
    depth = w_in.shape[0]
    n_p, l_p, d_model = x_prompt.shape
    n_s, l_s, _ = x_sample.shape
    w_in_b = _permute_a_heads(w_in, 2).astype(BF16)
    w_out_b = _permute_a_heads(w_out, 1).astype(BF16)
    up_b = mlp_up.astype(BF16)
    down_b = mlp_down.astype(BF16)
    sink = jnp.repeat(a_sink[:, jnp.array(A_PERM)], HEAD_DIM, axis=1).reshape(depth, 1, A_Q)
    cache_t = lambda c: jnp.transpose(c, (0, 1, 3, 4, 2))
    past = (cache_t(cache_a_k), cache_t(cache_a_v), cache_t(cache_b_k), cache_t(cache_b_v), state_c_wkv, state_c_shift)

    yp = x_prompt.reshape(n_p * l_p, d_model)
    ys = x_sample.reshape(n_s * l_s, d_model)
    p_new, s_new = [], []
    for l in range(depth):
        lw = {'norm1_g': norm1_g[l], 'norm2_g': norm2_g[l], 'w_in': w_in_b[l], 'w_out': w_out_b[l], 'sink': sink[l], 'a_sink': a_sink[l],
              'c_mu': c_mu[l], 'c_w0': c_w0[l], 'c_w2': c_w2[l], 'c_a0': c_a0[l], 'c_a2': c_a2[l], 'c_g2': c_g2[l],
              'c_k_k': c_k_k[l], 'c_k_a': c_k_a[l], 'c_r_k': c_r_k[l], 'c_ln_w': c_ln_w[l], 'c_ln_b': c_ln_b[l],
              'mlp_up': up_b[l], 'mlp_down': down_b[l]}
        yp, st_p = trunk_layer(yp, lw, n_p, None, l)
        ys, st_s = trunk_layer(ys, lw, n_s, past, l)
        p_new.append(st_p)
        s_new.append(st_s)
    p_state = [jnp.stack(t) for t in zip(*p_new)]
    s_state = [jnp.stack(t) for t in zip(*s_new)]
    y_prompt = final_norm(yp, final_norm_g, _row_tile(yp.shape[0])).reshape(n_p, l_p, d_model)
    y_sample = final_norm(ys, final_norm_g, _row_tile(ys.shape[0])).reshape(n_s, l_s, d_model)
    return (y_prompt, y_sample, *p_state, *s_state)
```

```python
import functools

import jax
import jax.numpy as jnp
from jax import lax
from jax.experimental import pallas as pl
from jax.experimental.pallas import tpu as pltpu

F32 = jnp.float32
BF16 = jnp.bfloat16

LANES = 128
HEAD_DIM = 64
BAND = 128
CHUNK = 64
RMS_EPS = 1e-5
GN_EPS = 64e-5
NEG = -1e30
VMEM_LIMIT = 56 * 1024 * 1024

A_Q, A_KV, B_W, C_W = 256, 128, 384, 384
C_PROJ = 1408
SPLITS = (0, 256, 384, 512, 896, 1280, 1664, 3072)
B_DILATIONS = (1, 4, 16)


def _cparams(sem):
    return pltpu.CompilerParams(dimension_semantics=sem, vmem_limit_bytes=VMEM_LIMIT)


def _dot(a, b):
    return jnp.dot(a.astype(BF16), b.astype(BF16), preferred_element_type=F32)


def _dot_nt(a, b):
    return lax.dot_general(a.astype(BF16), b.astype(BF16), (((1,), (1,)), ((), ())),
                           preferred_element_type=F32)


def _split(a):
    hi = a.astype(BF16)
    return hi, (a - hi.astype(F32)).astype(BF16)


def _dot_x2(a, b_exact):
    hi, lo = _split(a)
    b = b_exact.astype(BF16)
    return jnp.dot(hi, b, preferred_element_type=F32) + jnp.dot(lo, b, preferred_element_type=F32)


def _dot_x3(a, b):
    ah, al = _split(a)
    bh, bl = _split(b)
    return (jnp.dot(ah, bh, preferred_element_type=F32) + jnp.dot(al, bh, preferred_element_type=F32)
            + jnp.dot(ah, bl, preferred_element_type=F32))


def _iota(shape, dim):
    return lax.broadcasted_iota(jnp.int32, shape, dim)


def _norm_proj_kernel(x_ref, g_ref, w_ref, *out_refs):
    x = x_ref[...]
    h = x * lax.rsqrt(jnp.mean(x * x, axis=-1, keepdims=True) + RMS_EPS) * g_ref[...]
    hb = h.astype(BF16)
    for o_ref, a, b in zip(out_refs, SPLITS[:-1], SPLITS[1:]):
        o_ref[...] = jnp.dot(hb, w_ref[:, a:b], preferred_element_type=F32)


def norm_proj(x, g, w_bf16, tm):
    t, d = x.shape
    widths = [b - a for a, b in zip(SPLITS[:-1], SPLITS[1:])]
    return pl.pallas_call(
        _norm_proj_kernel,
        grid=(t // tm,),
        in_specs=[pl.BlockSpec((tm, d), lambda i: (i, 0)),
                  pl.BlockSpec((1, d), lambda i: (0, 0)),
                  pl.BlockSpec(w_bf16.shape, lambda i: (0, 0))],
        out_specs=[pl.BlockSpec((tm, w), lambda i: (i, 0)) for w in widths],
        out_shape=[jax.ShapeDtypeStruct((t, w), F32) for w in widths],
        compiler_params=_cparams(("parallel",)),
        name="norm_proj",
    )(x, g.reshape(1, d), w_bf16)


ATTN_QBLOCKS = 4


def _band_attn_kernel(*refs, nq, kv_of_q, has_sink, emit_lse, qblocks):
    q_ref, kp_ref, kc_ref, vp_ref, vc_ref = refs[:5]
    pos = 5
    sink_ref = None
    if has_sink:
        sink_ref = refs[pos]
        pos += 1
    o_ref = refs[pos]
    lse_ref = refs[pos + 1] if emit_lse else None

    i = pl.program_id(2)
    qi = _iota((2 * BAND, 2 * BAND), 0) % BAND
    kj = _iota((2 * BAND, 2 * BAND), 1)
    band = (kj >= qi) & (kj <= qi + BAND)
    band_first = band & ((kj >= BAND) | (i > 0))
    first_head = _iota((2 * BAND, 1), 0) < BAND
    lo = _iota((BAND, LANES), 1) < HEAD_DIM
    nkv = kp_ref.shape[1] // LANES
    lanes = lambda t: slice(t * LANES, (t + 1) * LANES)
    kcat = [jnp.concatenate([kp_ref[:, lanes(t)], kc_ref[:, lanes(t)]], axis=0).astype(BF16) for t in range(nkv)]
    vcat = [jnp.concatenate([vp_ref[:, lanes(t)], vc_ref[:, lanes(t)]], axis=0).astype(BF16) for t in range(nkv)]
    items = [(s, t) for s in range(qblocks) for t in range(nq)]
    rows = lambda s: slice(s * BAND, (s + 1) * BAND)
    keys = lambda s: slice(s * BAND, (s + 2) * BAND)

    scores = {}
    for s, t in items:
        q = q_ref[rows(s), lanes(t)] * (HEAD_DIM ** -0.5)
        q2 = jnp.concatenate([jnp.where(lo, q, 0.0), jnp.where(lo, 0.0, q)], axis=0)
        scores[s, t] = _dot_nt(q2, kcat[kv_of_q[t]][keys(s)])
    probs, dens, lses = {}, {}, {}
    for s, t in items:
        sc = jnp.where(band_first if s == 0 else band, scores[s, t], NEG)
        m = jnp.max(sc, axis=-1, keepdims=True)
        if has_sink:
            c = t * LANES
            sk = jnp.where(first_head, sink_ref[0:1, c:c + 1], sink_ref[0:1, c + HEAD_DIM:c + HEAD_DIM + 1])
            m = jnp.maximum(m, sk)
        p = jnp.exp(sc - m)
        den = jnp.sum(p, axis=-1, keepdims=True)
        if has_sink:
            den = den + jnp.exp(sk - m)
        probs[s, t], dens[s, t] = p, den
        if emit_lse:
            lses[s, t] = m + jnp.log(den)
    for s, t in items:
        o2 = _dot(probs[s, t], vcat[kv_of_q[t]][keys(s)]) / dens[s, t]
        o_ref[rows(s), lanes(t)] = jnp.where(lo, o2[:BAND], o2[BAND:])
        if emit_lse:
            l2 = lses[s, t]
            lse_ref[rows(s), lanes(t)] = jnp.where(lo, l2[:BAND], l2[BAND:])


def band_attn(q, k, v, *, n, d, kv_of_q, sink=None, emit_lse=False):
    t, wq = q.shape
    wkv = k.shape[1]
    nq = wq // LANES
    nb = t // n // d // BAND
    qblocks = min(ATTN_QBLOCKS, nb)
    steps = nb // qblocks
    qv = q.reshape(t // d, d * wq)
    kv = k.reshape(t // d, d * wkv)
    vv = v.reshape(t // d, d * wkv)
    cur = lambda b, r, i: (b * steps + i, r)
    prev = lambda b, r, i: (b * nb + jnp.maximum(i * qblocks - 1, 0), r)
    in_specs = [pl.BlockSpec((qblocks * BAND, wq), cur),
                pl.BlockSpec((BAND, wkv), prev), pl.BlockSpec((qblocks * BAND, wkv), cur),
                pl.BlockSpec((BAND, wkv), prev), pl.BlockSpec((qblocks * BAND, wkv), cur)]
    args = [qv, kv, kv, vv, vv]
    if sink is not None:
        in_specs.append(pl.BlockSpec((1, wq), lambda b, r, i: (0, 0)))
        args.append(sink)
    n_out = 2 if emit_lse else 1
    outs = pl.pallas_call(
        functools.partial(_band_attn_kernel, nq=nq, kv_of_q=kv_of_q, has_sink=sink is not None,
                          emit_lse=emit_lse, qblocks=qblocks),
        grid=(n, d, steps),
        in_specs=in_specs,
        out_specs=[pl.BlockSpec((qblocks * BAND, wq), cur)] * n_out,
        out_shape=[jax.ShapeDtypeStruct((t // d, d * wq), F32)] * n_out,
        compiler_params=_cparams(("parallel", "parallel", "arbitrary")),
        name="band_attn_d%d" % d,
    )(*args)
    return [o.reshape(t, wq) for o in outs]


def _merge_kernel(o1, o2, o3, l1, l2, l3, out_ref):
    a, b, c = l1[...], l2[...], l3[...]
    m = jnp.maximum(jnp.maximum(a, b), c)
    wa, wb, wc = jnp.exp(a - m), jnp.exp(b - m), jnp.exp(c - m)
    out_ref[...] = (wa * o1[...] + wb * o2[...] + wc * o3[...]) / (wa + wb + wc)


def merge_patterns(outs, lses, tm):
    t, w = outs[0].shape
    spec = pl.BlockSpec((tm, w), lambda i: (i, 0))
    return pl.pallas_call(
        _merge_kernel, grid=(t // tm,), in_specs=[spec] * 6, out_specs=spec,
        out_shape=jax.ShapeDtypeStruct((t, w), F32),
        compiler_params=_cparams(("parallel",)), name="merge_patterns",
    )(*outs, *lses)


NEW_PAD = 8


def _softmax_t(s_c, s_n, valid_c, valid_n, sink):
    s_c = jnp.where(valid_c, s_c, NEG)
    s_n = jnp.where(valid_n, s_n, NEG)
    m = jnp.maximum(jnp.max(s_c, axis=-1, keepdims=True), jnp.max(s_n, axis=-1, keepdims=True))
    if sink is not None:
        m = jnp.maximum(m, sink)
    p_c = jnp.exp(s_c - m)
    p_n = jnp.exp(s_n - m)
    den = jnp.sum(p_c, axis=-1, keepdims=True) + jnp.sum(p_n, axis=-1, keepdims=True)
    if sink is not None:
        den = den + jnp.exp(sink - m)
    return p_c, p_n, den, m + jnp.log(den)


def _sample_attn_kernel(qa_ref, kan_ref, van_ref, cak_ref, cav_ref, sink_ref,
                        qb_ref, kbn_ref, vbn_ref, cbk_ref, cbv_ref, oa_ref, ob_ref, *, n_new):
    pa = cak_ref.shape[-1]
    pb = cbk_ref.shape[-1]
    groups_a, heads_b = cak_ref.shape[2], cbk_ref.shape[2]
    scale = HEAD_DIM ** -0.5
    new_i = _iota((NEW_PAD, LANES), 1)
    live_n = new_i < n_new
    jn = _iota((NEW_PAD, LANES), 0)

    qa = [(qa_ref[0, g] * scale).astype(BF16) for g in range(groups_a)]
    qb = [(qb_ref[0, h] * scale).astype(BF16) for h in range(heads_b)]
    sa_c = [_dot(qa[g], cak_ref[0, 0, g]) for g in range(groups_a)]
    sa_n = [_dot(qa[g], kan_ref[0, g]) for g in range(groups_a)]
    sb_c = [_dot(qb[h], cbk_ref[0, 0, h]) for h in range(heads_b)]
    sb_n = [_dot(qb[h], kbn_ref[0, h]) for h in range(heads_b)]

    valid_c = _iota((NEW_PAD, pa), 1) >= _iota((NEW_PAD, pa), 0) % n_new
    valid_n = live_n & (new_i <= jn % n_new)
    soft_a = [_softmax_t(sa_c[g], sa_n[g], valid_c, valid_n, sink_ref[g][:, 0:1]) for g in range(groups_a)]
    soft_b = {}
    for d in B_DILATIONS:
        span = BAND * d
        jc = _iota((NEW_PAD, span), 0)
        pc = _iota((NEW_PAD, span), 1)
        if d == 1:
            valid_c, valid_n = pc >= jc, live_n & (new_i <= jn)
        else:
            valid_c, valid_n = pc % d == jc, live_n & (new_i == jn)
        for h in range(heads_b):
            soft_b[h, d] = _softmax_t(sb_c[h][:, pb - span:], sb_n[h], valid_c, valid_n, None)

    for g in range(groups_a):
        p_c, p_n, den, _ = soft_a[g]
        oa_ref[0, g] = (_dot_nt(p_c, cav_ref[0, 0, g]) + _dot_nt(p_n, van_ref[0, g])) / den
    outs = {}
    for h in range(heads_b):
        for d in B_DILATIONS:
            p_c, p_n, den, _ = soft_b[h, d]
            outs[h, d] = (_dot_nt(p_c, cbv_ref[0, 0, h, :, pb - BAND * d:]) + _dot_nt(p_n, vbn_ref[0, h])) / den
    for h in range(heads_b):
        lses = [soft_b[h, d][3] for d in B_DILATIONS]
        m = jnp.maximum(jnp.maximum(lses[0], lses[1]), lses[2])
        ws = [jnp.exp(l - m) for l in lses]
        o1, o2, o3 = [outs[h, d] for d in B_DILATIONS]
        ob_ref[0, h] = (ws[0] * o1 + ws[1] * o2 + ws[2] * o3) / (ws[0] + ws[1] + ws[2])


def _seg_ones(w, dtype):
    r = jnp.arange(w) // HEAD_DIM
    return (r[:, None] == r[None, :]).astype(dtype)


def _new_keys_t(x, n, l, heads):
    xt = x.reshape(n, l, heads, HEAD_DIM).transpose(0, 2, 3, 1)
    return jnp.pad(xt, ((0, 0), (0, 0), (0, 0), (0, LANES - l)))


def sample_attn(qa, ka, va, cache_a_k_t, cache_a_v_t, sink, qb, kb, vb, cache_b_k_t, cache_b_v_t, layer, n, l):
    ga, gb = cache_a_k_t.shape[2], cache_b_k_t.shape[2]
    pa, pb = cache_a_k_t.shape[4], cache_b_k_t.shape[4]
    reps = A_Q // A_KV
    assert pa == BAND and pb == BAND * B_DILATIONS[-1] and l * reps == NEW_PAD and l <= B_DILATIONS[1]
    qa_g = qa.reshape(n, l, reps, ga, HEAD_DIM).transpose(0, 3, 2, 1, 4).reshape(n, ga, NEW_PAD, HEAD_DIM)
    sink_g = jnp.broadcast_to(sink.reshape(ga, reps, 1, 1), (ga, reps, l, LANES)).reshape(ga, NEW_PAD, LANES)
    qb_g = jnp.pad(qb.reshape(n, l, gb, HEAD_DIM).transpose(0, 2, 1, 3), ((0, 0), (0, 0), (0, NEW_PAD - l), (0, 0)))
    rows = lambda g: pl.BlockSpec((1, g, NEW_PAD, HEAD_DIM), lambda b: (b, 0, 0, 0))
    cols = lambda g: pl.BlockSpec((1, g, HEAD_DIM, LANES), lambda b: (b, 0, 0, 0))
    cache = lambda g, p: pl.BlockSpec((1, 1, g, HEAD_DIM, p), lambda b: (layer, b, 0, 0, 0))
    oa, ob = pl.pallas_call(
        functools.partial(_sample_attn_kernel, n_new=l),
        grid=(n,),
        in_specs=[rows(ga), cols(ga), cols(ga), cache(ga, pa), cache(ga, pa),
                  pl.BlockSpec((ga, NEW_PAD, LANES), lambda b: (0, 0, 0)),
                  rows(gb), cols(gb), cols(gb), cache(gb, pb), cache(gb, pb)],
        out_specs=[rows(ga), rows(gb)],
        out_shape=[jax.ShapeDtypeStruct((n, ga, NEW_PAD, HEAD_DIM), F32),
                   jax.ShapeDtypeStruct((n, gb, NEW_PAD, HEAD_DIM), F32)],
        compiler_params=_cparams(("parallel",)),
        name="sample_attn",
    )(qa_g, _new_keys_t(ka, n, l, ga), _new_keys_t(va, n, l, ga), cache_a_k_t, cache_a_v_t, sink_g,
      qb_g, _new_keys_t(kb, n, l, gb), _new_keys_t(vb, n, l, gb), cache_b_k_t, cache_b_v_t)
    oa = oa.reshape(n, ga, reps, l, HEAD_DIM).transpose(0, 3, 2, 1, 4).reshape(n * l, A_Q)
    ob = ob[:, :, :l].transpose(0, 2, 1, 3).reshape(n * l, B_W)
    return oa, ob


def _rwkv_prep_kernel(z_ref, zfirst_ref, mu_ref, w0_ref, a0_ref, kk_ref, ka_ref, w2_ref, a2_ref, g2_ref, seg_ref,
                      r_out, k_out, v_out, al_out, b_out, lw_out, g_out, carry_ref, *, seq_len):
    tile = pl.program_id(1)
    z = z_ref[...]
    tm = z.shape[0]
    rolled = pltpu.roll(z, 1, axis=0)
    if seq_len >= tm:
        @pl.when(tile == 0)
        def _():
            carry_ref[...] = zfirst_ref[0]

        prev = jnp.where(_iota((tm, 1), 0) == 0, carry_ref[...], rolled)
        carry_ref[...] = z[tm - 1:tm, :]
    else:
        prev = jnp.where(_iota((tm, 1), 0) % seq_len == 0, zfirst_ref[...], rolled)
    zs = z + (prev - z) * mu_ref[...]
    r = zs[:, 0:C_W]
    k = zs[:, C_W:2 * C_W]
    v = zs[:, 2 * C_W:3 * C_W]
    lora = zs[:, 3 * C_W:3 * C_W + LANES]
    zg = zs[:, 3 * C_W + LANES:]
    w_log = -jax.nn.softplus(-(w0_ref[...] + _dot_x3(jnp.tanh(lora), w2_ref[...]))) - 0.5
    a = jax.nn.sigmoid(a0_ref[...] + _dot_x3(lora, a2_ref[...]))
    g = _dot_x3(jax.nn.sigmoid(zg), g2_ref[...])
    kk = k * kk_ref[...]
    kk = kk / jnp.maximum(jnp.sqrt(_dot_x2(kk * kk, seg_ref[...])), 1e-12)
    r_out[...] = r
    k_out[...] = k * (1.0 + (a - 1.0) * ka_ref[...])
    v_out[...] = v
    al_out[...] = -kk
    b_out[...] = kk * a
    lw_out[...] = -jnp.exp(w_log)
    g_out[...] = g


def rwkv_prep(zc, z_before, lw, n_seq):
    t = zc.shape[0]
    seq_len = t // n_seq
    tm = min(PREP_ROWS, t)
    if seq_len >= tm:
        grid = (n_seq, seq_len // tm)
        zfirst = z_before.reshape(n_seq, 1, C_PROJ)
        zfirst_spec = pl.BlockSpec((1, 1, C_PROJ), lambda b, i: (b, 0, 0))
    else:
        grid = (1, t // tm)
        zfirst = jnp.pad(z_before.reshape(n_seq, 1, C_PROJ), ((0, 0), (0, seq_len - 1), (0, 0))).reshape(t, C_PROJ)
        zfirst_spec = pl.BlockSpec((tm, C_PROJ), lambda b, i: (i, 0))
    tiles = grid[1]
    row = lambda b, i: (b * tiles + i, 0)
    const = lambda s: pl.BlockSpec(s, lambda b, i: (0, 0))
    vec = lambda a: a.reshape(1, -1)
    zero64 = jnp.zeros((HEAD_DIM, C_W), F32)
    w2p = jnp.concatenate([lw['c_w2'], zero64], axis=0)
    a2p = jnp.concatenate([zero64, lw['c_a2']], axis=0)
    out_spec = pl.BlockSpec((tm, C_W), row)
    return pl.pallas_call(
        functools.partial(_rwkv_prep_kernel, seq_len=seq_len),
        grid=grid,
        in_specs=[pl.BlockSpec((tm, C_PROJ), row),
                  zfirst_spec,
                  const((1, C_PROJ)), const((1, C_W)), const((1, C_W)), const((1, C_W)), const((1, C_W)),
                  const((LANES, C_W)), const((LANES, C_W)), const((LANES, C_W)), const((C_W, C_W))],
        out_specs=[out_spec] * 7,
        out_shape=[jax.ShapeDtypeStruct((t, C_W), F32)] * 7,
        scratch_shapes=[pltpu.VMEM((1, C_PROJ), F32)],
        compiler_params=_cparams(("parallel", "arbitrary")),
        name="rwkv_prep",
    )(zc, zfirst, vec(lw['c_mu']), vec(lw['c_w0']), vec(lw['c_a0']), vec(lw['c_k_k']), vec(lw['c_k_a']),
      w2p, a2p, lw['c_g2'], _seg_ones(C_W, F32))


SCAN_GROUP = 8
SCAN_PACKED_GROUP = 2


def _pair_block_diag(x, mask_b):
    xb = x.astype(BF16)
    return jnp.concatenate([xb, xb], axis=0) * mask_b


def _rwkv_scan_kernel(r_ref, k_ref, v_ref, al_ref, b_ref, lw_ref, g_ref, s0_ref, rk_ref, lnw_ref, lnb_ref, seg_ref,
                      o_ref, sfin_ref, s_ref, *, seq_len, group):
    step = pl.program_id(1)
    pairs = C_W // LANES
    rows = group * CHUNK
    chain = seq_len >= CHUNK
    span = min(seq_len, CHUNK)
    per = CHUNK // span
    ri = _iota((LANES, LANES), 0)
    ci = _iota((LANES, LANES), 1)
    bd_mask = (ri // HEAD_DIM) == (ci // HEAD_DIM)
    mask_b = jnp.where(bd_mask, 1.0, 0.0).astype(BF16)
    bd = lambda x: _pair_block_diag(x, mask_b)

    if chain:
        @pl.when(step == 0)
        def _():
            s_ref[...] = s0_ref[0]

    r, k, v, al, b, lw = r_ref[...], k_ref[...], v_ref[...], al_ref[...], b_ref[...], lw_ref[...]

    ti = _iota((rows, rows), 0)
    si = _iota((rows, rows), 1)
    same = ti // span == si // span
    tri = jnp.where((si <= ti) & same, 1.0, 0.0).astype(BF16)
    ones = jnp.where(same, 1.0, 0.0).astype(BF16)
    lw_hi, lw_lo = _split(lw)
    cum = jnp.dot(tri, lw_hi, preferred_element_type=F32) + jnp.dot(tri, lw_lo, preferred_element_type=F32)
    cum_end = jnp.dot(ones, lw_hi, preferred_element_type=F32) + jnp.dot(ones, lw_lo, preferred_element_type=F32)
    e_neg = jnp.exp(-cum)
    e_rem = jnp.exp(cum_end - cum)
    a_t = al * jnp.exp(cum - lw)
    b_t = b * e_neg
    k_t = k * e_neg
    r_t = r * jnp.exp(cum)
    b_h = b * e_rem
    k_h = k * e_rem

    tt = _iota((CHUNK, LANES), 0)
    ss = _iota((CHUNK, LANES), 1) % CHUNK
    together = tt // span == ss // span
    strict = (tt > ss) & together
    incl = (tt >= ss) & together
    eye = (tt == ss).astype(F32)
    items = [(c, m) for c in range(group) for m in range(pairs)]
    tile = lambda x, it: x[it[0] * CHUNK:(it[0] + 1) * CHUNK, it[1] * LANES:(it[1] + 1) * LANES]

    gram = {}
    for it in items:
        ar = jnp.concatenate([tile(a_t, it), tile(r_t, it)], axis=0)
        bk = jnp.concatenate([bd(tile(b_t, it)), bd(tile(k_t, it))], axis=0)
        gram[it] = _dot_nt(ar, bk)
    m_ab = {it: jnp.where(strict, gram[it][:CHUNK, :LANES], 0.0) for it in items}
    n_rb = {it: jnp.where(incl, gram[it][CHUNK:, :LANES], 0.0) for it in items}
    m_ak = {it: jnp.where(strict, gram[it][:CHUNK, LANES:], 0.0) for it in items}
    n_rk = {it: jnp.where(incl, gram[it][CHUNK:, LANES:], 0.0) for it in items}
    t_inv = {it: eye + jnp.where(tt // 2 == ss // 2, m_ab[it], 0.0) for it in items}
    blk = 4
    while blk <= span:
        joins = (tt // blk == ss // blk) & (tt // (blk // 2) != ss // (blk // 2))
        de = {it: _dot(t_inv[it], bd(jnp.where(joins, m_ab[it], 0.0))) for it in items}
        t_inv = {it: t_inv[it] + _dot(de[it], bd(t_inv[it])) for it in items}
        blk *= 2
    wv = {it: _dot(jnp.concatenate([m_ak[it], n_rk[it]], axis=0), bd(tile(v, it))) for it in items}
    pq = {it: _dot(t_inv[it], jnp.concatenate([bd(tile(a_t, it)), bd(wv[it][:CHUNK])], axis=1)) for it in items}

    y = {}
    if chain:
        state = [s_ref[m] for m in range(pairs)]
        for c in range(group):
            for m in range(pairs):
                it = (c, m)
                uy = _dot_nt(jnp.concatenate([pq[it][:, :LANES], tile(r_t, it)], axis=0), state[m])
                u = uy[:CHUNK] + pq[it][:, LANES:]
                y[it] = uy[CHUNK:] + _dot(n_rb[it], bd(u)) + wv[it][CHUNK:]
                upd = _dot(jnp.concatenate([u, tile(v, it)], axis=0).T,
                           jnp.concatenate([tile(b_h, it), tile(k_h, it)], axis=0))
                g_end = jnp.exp(tile(cum_end, it)[0:1, :])
                state[m] = state[m] * g_end + jnp.where(bd_mask, upd, 0.0)
        for m in range(pairs):
            s_ref[m] = state[m]
    else:
        row_seq = _iota((2 * CHUNK, 1), 0) % CHUNK // span
        col_seq = _iota((LANES, 2 * CHUNK), 1) % CHUNK // span
        for it in items:
            c, m = it
            pr = jnp.concatenate([pq[it][:, :LANES], tile(r_t, it)], axis=0)
            uy = None
            for j in range(per):
                part = _dot_nt(jnp.where(row_seq == j, pr, 0.0), s0_ref[c * per + j, m])
                uy = part if uy is None else uy + part
            u = uy[:CHUNK] + pq[it][:, LANES:]
            y[it] = uy[CHUNK:] + _dot(n_rb[it], bd(u)) + wv[it][CHUNK:]
            uv_t = jnp.concatenate([u, tile(v, it)], axis=0).T
            bk = jnp.concatenate([tile(b_h, it), tile(k_h, it)], axis=0).astype(BF16)
            for j in range(per):
                upd = _dot(jnp.where(col_seq == j, uv_t, 0.0), bk)
                g_end = jnp.exp(tile(cum_end, it)[j * span:j * span + 1, :])
                sfin_ref[c * per + j, m] = s0_ref[c * per + j, m] * g_end + jnp.where(bd_mask, upd, 0.0)

    seg = seg_ref[...]
    for m in range(pairs):
        sl = slice(m * LANES, (m + 1) * LANES)
        ym = jnp.concatenate([y[(c, m)] for c in range(group)], axis=0)
        mu = _dot_x2(ym, seg) * (1.0 / HEAD_DIM)
        dy = ym - mu
        var = _dot_x2(dy * dy, seg) * (1.0 / HEAD_DIM)
        yn = dy * lax.rsqrt(var + GN_EPS) * lnw_ref[:, sl] + lnb_ref[:, sl]
        bonus = _dot_x2(r[:, sl] * k[:, sl] * rk_ref[:, sl], seg)
        o_ref[:, sl] = (yn + bonus * v[:, sl]) * g_ref[:, sl]

    if chain:
        @pl.when(step == pl.num_programs(1) - 1)
        def _():
            sfin_ref[0] = s_ref[...]


def _state_to_pairs(s):
    n, h, hd, _ = s.shape
    s = s.reshape(n, h // 2, 2, hd, hd)
    bd = s[:, :, :, :, None, :] * jnp.eye(2, dtype=s.dtype)[None, None, :, None, :, None]
    return bd.reshape(n, h // 2, 2 * hd, 2 * hd)


def _pairs_to_state(bd):
    n, p, w, _ = bd.shape
    hd = w // 2
    bd = bd.reshape(n, p, 2, hd, 2, hd)
    return jnp.stack([bd[:, :, 0, :, 0, :], bd[:, :, 1, :, 1, :]], axis=2).reshape(n, 2 * p, hd, hd)


def rwkv_scan(prep, s0, lw, n_seq):
    t = prep[0].shape[0]
    seq_len = t // n_seq
    pairs = C_W // LANES
    if seq_len >= CHUNK:
        group = min(SCAN_GROUP, seq_len // CHUNK)
        grid = (n_seq, seq_len // CHUNK // group)
        n_state = 1
    else:
        group = min(SCAN_PACKED_GROUP, t // CHUNK)
        grid = (t // CHUNK // group, 1)
        n_state = group * (CHUNK // seq_len)
    steps = grid[1]
    row = pl.BlockSpec((group * CHUNK, C_W), lambda b, c: (b * steps + c, 0))
    const = lambda s: pl.BlockSpec(s, lambda b, c: (0, 0))
    state = pl.BlockSpec((n_state, pairs, LANES, LANES), lambda b, c: (b, 0, 0, 0))
    vec = lambda a: a.reshape(1, -1)
    out, s_fin = pl.pallas_call(
        functools.partial(_rwkv_scan_kernel, seq_len=seq_len, group=group),
        grid=grid,
        in_specs=[row] * 7 + [state, const((1, C_W)), const((1, C_W)), const((1, C_W)), const((LANES, LANES))],
        out_specs=[row, state],
        out_shape=[jax.ShapeDtypeStruct((t, C_W), F32), jax.ShapeDtypeStruct((n_seq, pairs, LANES, LANES), F32)],
        scratch_shapes=[pltpu.VMEM((pairs, LANES, LANES), F32)],
        compiler_params=_cparams(("parallel", "arbitrary")),
        name="rwkv_scan",
    )(*prep, _state_to_pairs(s0), vec(lw['c_r_k']), vec(lw['c_ln_w']), vec(lw['c_ln_b']), _seg_ones(LANES, F32))
    return out, _pairs_to_state(s_fin)


def _out_proj_kernel(x_ref, oa_ref, ob_ref, oc_ref, w_ref, y_ref):
    acc = x_ref[...]
    acc += jnp.dot(oa_ref[...].astype(BF16), w_ref[0:A_Q, :], preferred_element_type=F32)
    acc += jnp.dot(ob_ref[...].astype(BF16), w_ref[A_Q:A_Q + B_W, :], preferred_element_type=F32)
    acc += jnp.dot(oc_ref[...].astype(BF16), w_ref[A_Q + B_W:, :], preferred_element_type=F32)
    y_ref[...] = acc


def out_proj(x, oa, ob, oc, w_bf16, tm):
    t, d = x.shape
    row = lambda w: pl.BlockSpec((tm, w), lambda i: (i, 0))
    return pl.pallas_call(
        _out_proj_kernel, grid=(t // tm,),
        in_specs=[row(d), row(A_Q), row(B_W), row(C_W), pl.BlockSpec(w_bf16.shape, lambda i: (0, 0))],
        out_specs=row(d), out_shape=jax.ShapeDtypeStruct((t, d), F32),
        compiler_params=_cparams(("parallel",)), name="out_proj",
    )(x, oa, ob, oc, w_bf16)


FF_CHUNK = 1024


def _mlp_kernel(x_ref, g_ref, up_ref, down_ref, y_ref):
    x = x_ref[...]
    h = (x * lax.rsqrt(jnp.mean(x * x, axis=-1, keepdims=True) + RMS_EPS) * g_ref[...]).astype(BF16)
    acc = x
    for c in range(up_ref.shape[1] // FF_CHUNK):
        u = jnp.dot(h, up_ref[:, c * FF_CHUNK:(c + 1) * FF_CHUNK], preferred_element_type=F32)
        u = jnp.square(jnp.maximum(u, 0.0)).astype(BF16)
        acc += jnp.dot(u, down_ref[c * FF_CHUNK:(c + 1) * FF_CHUNK, :], preferred_element_type=F32)
    y_ref[...] = acc


def mlp(x, g, up_bf16, down_bf16, tm):
    t, d = x.shape
    row = pl.BlockSpec((tm, d), lambda i: (i, 0))
    return pl.pallas_call(
        _mlp_kernel, grid=(t // tm,),
        in_specs=[row, pl.BlockSpec((1, d), lambda i: (0, 0)),
                  pl.BlockSpec(up_bf16.shape, lambda i: (0, 0), pipeline_mode=pl.Buffered(1)),
                  pl.BlockSpec(down_bf16.shape, lambda i: (0, 0), pipeline_mode=pl.Buffered(1))],
        out_specs=row, out_shape=jax.ShapeDtypeStruct((t, d), F32),
        compiler_params=_cparams(("parallel",)), name="mlp",
    )(x, g.reshape(1, d), up_bf16, down_bf16)


def _final_norm_kernel(x_ref, g_ref, y_ref):
    x = x_ref[...]
    y_ref[...] = x * lax.rsqrt(jnp.mean(x * x, axis=-1, keepdims=True) + RMS_EPS) * g_ref[...]


def final_norm(x, g, tm):
    t, d = x.shape
    row = pl.BlockSpec((tm, d), lambda i: (i, 0))
    return pl.pallas_call(
        _final_norm_kernel, grid=(t // tm,),
        in_specs=[row, pl.BlockSpec((1, d), lambda i: (0, 0))],
        out_specs=row, out_shape=jax.ShapeDtypeStruct((t, d), F32),
        compiler_params=_cparams(("parallel",)), name="final_norm",
    )(x, g.reshape(1, d))


PREP_ROWS = 256
A_PERM = (0, 2, 1, 3)


def _permute_a_heads(w, axis):
    take = lambda a, b: lax.slice_in_dim(w, a, b, axis=axis)
    parts = [take(h * HEAD_DIM, (h + 1) * HEAD_DIM) for h in A_PERM] + [take(A_Q, w.shape[axis])]
    return jnp.concatenate(parts, axis=axis)


def _row_tile(t):
    return 512 if t % 512 == 0 else t


def trunk_layer(x, lw, n, past, layer):
    t = x.shape[0]
    l = t // n
    tm = _row_tile(t)
    qa, ka, va, qb, kb, vb, zc = norm_proj(x, lw['norm1_g'], lw['w_in'], tm)
    if past is None:
        oa, = band_attn(qa, ka, va, n=n, d=1, kv_of_q=(0, 0), sink=lw['sink'])
        outs, lses = [], []
        for d in B_DILATIONS:
            o, lse = band_attn(qb, kb, vb, n=n, d=d, kv_of_q=(0, 1, 2), emit_lse=True)
            outs.append(o)
            lses.append(lse)
        ob = merge_patterns(outs, lses, tm)
        z_before = jnp.zeros((n, C_PROJ), F32)
        s0 = jnp.zeros((n, C_W // HEAD_DIM, HEAD_DIM, HEAD_DIM), F32)
    else:
        cache_a_k_t, cache_a_v_t, cache_b_k_t, cache_b_v_t, state_wkv, state_shift = past
        oa, ob = sample_attn(qa, ka, va, cache_a_k_t, cache_a_v_t, lw['a_sink'],
                             qb, kb, vb, cache_b_k_t, cache_b_v_t, layer, n, l)
        z_before, s0 = state_shift[layer], state_wkv[layer]
    oc, s_fin = rwkv_scan(rwkv_prep(zc, z_before, lw, n), s0, lw, n)
    x = out_proj(x, oa, ob, oc, lw['w_out'], tm)
    x = mlp(x, lw['norm2_g'], lw['mlp_up'], lw['mlp_down'], tm)
    heads = lambda a, h: a.reshape(n, l, h, HEAD_DIM)
    ra, rb = min(BAND, l), min(BAND * B_DILATIONS[-1], l)
    new_state = (heads(ka, 2)[:, l - ra:], heads(va, 2)[:, l - ra:], heads(kb, 6)[:, l - rb:], heads(vb, 6)[:, l - rb:],
                 s_fin, zc.reshape(n, l, C_PROJ)[:, -1])
    return x, new_state


def kernel(x_prompt, x_sample, cache_a_k, cache_a_v, cache_b_k, cache_b_v, state_c_wkv, state_c_shift, norm1_g, norm2_g, w_in, w_out, a_sink, c_mu, c_w0, c_w2, c_a0, c_a2, c_g2, c_k_k, c_k_a, c_r_k, c_ln_w, c_ln_b, mlp_up, mlp_down, final_norm_g):
    depth = w_in.shape[0]
    n_p, l_p, d_model = x_prompt.shape
    n_s, l_s, _ = x_sample.shape
    w_in_b = _permute_a_heads(w_in, 2).astype(BF16)
    w_out_b = _permute_a_heads(w_out, 1).astype(BF16)
    up_b = mlp_up.astype(BF16)
    down_b = mlp_down.astype(BF16)
    sink = jnp.repeat(a_sink[:, jnp.array(A_PERM)], HEAD_DIM, axis=1).reshape(depth, 1, A_Q)
    cache_t = lambda c: jnp.transpose(c, (0, 1, 3, 4, 2))
    past = (cache_t(cache_a_k), cache_t(cache_a_v), cache_t(cache_b_k), cache_t(cache_b_v), state_c_wkv, state_c_shift)

    yp = x_prompt.reshape(n_p * l_p, d_model)
    ys = x_sample.reshape(n_s * l_s, d_model)
    p_new, s_new = [], []
    for l in range(depth):
        lw = {'norm1_g': norm1_g[l], 'norm2_g': norm2_g[l], 'w_in': w_in_b[l], 'w_out': w_out_b[l], 'sink': sink[l], 'a_sink': a_sink[l],
              'c_mu': c_mu[l], 'c_w0': c_w0[l], 'c_w2': c_w2[l], 'c_a0': c_a0[l], 'c_a2': c_a2[l], 'c_g2': c_g2[l],
              'c_k_k': c_k_k[l], 'c_k_a': c_k_a[l], 'c_r_k': c_r_k[l], 'c_ln_w': c_ln_w[l], 'c_ln_b': c_ln_b[l],
              'mlp_up': up_b[l], 'mlp_down': down_b[l]}
        yp, st_p = trunk_layer(yp, lw, n_p, None, l)
        ys, st_s = trunk_layer(ys, lw, n_s, past, l)
        p_new.append(st_p)
        s_new.append(st_s)
    p_state = [jnp.stack(t) for t in zip(*p_new)]
    s_state = [jnp.stack(t) for t in zip(*s_new)]
    y_prompt = final_norm(yp, final_norm_g, _row_tile(yp.shape[0])).reshape(n_p, l_p, d_model)
    y_sample = final_norm(ys, final_norm_g, _row_tile(ys.shape[0])).reshape(n_s, l_s, d_model)
    return (y_prompt, y_sample, *p_state, *s_state)
```

```python
import functools

import jax
import jax.numpy as jnp
from jax import lax
from jax.experimental import pallas as pl
from jax.experimental.pallas import tpu as pltpu

F32 = jnp.float32
BF16 = jnp.bfloat16

LANES = 128
HEAD_DIM = 64
BAND = 128
CHUNK = 64
RMS_EPS = 1e-5
GN_EPS = 64e-5
NEG = -1e30
VMEM_LIMIT = 56 * 1024 * 1024

A_Q, A_KV, B_W, C_W = 256, 128, 384, 384
C_PROJ = 1408
SPLITS = (0, 256, 384, 512, 896, 1280, 1664, 3072)
B_DILATIONS = (1, 4, 16)


def _cparams(sem):
    return pltpu.CompilerParams(dimension_semantics=sem, vmem_limit_bytes=VMEM_LIMIT)


def _dot(a, b):
    return jnp.dot(a.astype(BF16), b.astype(BF16), preferred_element_type=F32)


def _dot_nt(a, b):
    return lax.dot_general(a.astype(BF16), b.astype(BF16), (((1,), (1,)), ((), ())),
                           preferred_element_type=F32)


def _split(a):
    hi = a.astype(BF16)
    return hi, (a - hi.astype(F32)).astype(BF16)


def _dot_x2(a, b_exact):
    hi, lo = _split(a)
    b = b_exact.astype(BF16)
    return jnp.dot(hi, b, preferred_element_type=F32) + jnp.dot(lo, b, preferred_element_type=F32)


def _dot_x3(a, b):
    ah, al = _split(a)
    bh, bl = _split(b)
    return (jnp.dot(ah, bh, preferred_element_type=F32) + jnp.dot(al, bh, preferred_element_type=F32)
            + jnp.dot(ah, bl, preferred_element_type=F32))


def _iota(shape, dim):
    return lax.broadcasted_iota(jnp.int32, shape, dim)


def _norm_proj_kernel(x_ref, g_ref, w_ref, *out_refs):
    x = x_ref[...]
    h = x * lax.rsqrt(jnp.mean(x * x, axis=-1, keepdims=True) + RMS_EPS) * g_ref[...]
    hb = h.astype(BF16)
    for o_ref, a, b in zip(out_refs, SPLITS[:-1], SPLITS[1:]):
        o_ref[...] = jnp.dot(hb, w_ref[:, a:b], preferred_element_type=F32)


def norm_proj(x, g, w_bf16, tm):
    t, d = x.shape
    widths = [b - a for a, b in zip(SPLITS[:-1], SPLITS[1:])]
    return pl.pallas_call(
        _norm_proj_kernel,
        grid=(t // tm,),
        in_specs=[pl.BlockSpec((tm, d), lambda i: (i, 0)),
                  pl.BlockSpec((1, d), lambda i: (0, 0)),
                  pl.BlockSpec(w_bf16.shape, lambda i: (0, 0))],
        out_specs=[pl.BlockSpec((tm, w), lambda i: (i, 0)) for w in widths],
        out_shape=[jax.ShapeDtypeStruct((t, w), F32) for w in widths],
        compiler_params=_cparams(("parallel",)),
        name="norm_proj",
    )(x, g.reshape(1, d), w_bf16)


ATTN_QBLOCKS = 4


def _band_attn_kernel(*refs, nq, kv_of_q, has_sink, emit_lse, qblocks):
    q_ref, kp_ref, kc_ref, vp_ref, vc_ref = refs[:5]
    pos = 5
    sink_ref = None
    if has_sink:
        sink_ref = refs[pos]
        pos += 1
    o_ref = refs[pos]
    lse_ref = refs[pos + 1] if emit_lse else None

    i = pl.program_id(2)
    qi = _iota((2 * BAND, 2 * BAND), 0) % BAND
    kj = _iota((2 * BAND, 2 * BAND), 1)
    band = (kj >= qi) & (kj <= qi + BAND)
    band_first = band & ((kj >= BAND) | (i > 0))
    first_head = _iota((2 * BAND, 1), 0) < BAND
    lo = _iota((BAND, LANES), 1) < HEAD_DIM
    nkv = kp_ref.shape[1] // LANES
    lanes = lambda t: slice(t * LANES, (t + 1) * LANES)
    kcat = [jnp.concatenate([kp_ref[:, lanes(t)], kc_ref[:, lanes(t)]], axis=0).astype(BF16) for t in range(nkv)]
    vcat = [jnp.concatenate([vp_ref[:, lanes(t)], vc_ref[:, lanes(t)]], axis=0).astype(BF16) for t in range(nkv)]
    items = [(s, t) for s in range(qblocks) for t in range(nq)]
    rows = lambda s: slice(s * BAND, (s + 1) * BAND)
    keys = lambda s: slice(s * BAND, (s + 2) * BAND)

    scores = {}
    for s, t in items:
        q = q_ref[rows(s), lanes(t)] * (HEAD_DIM ** -0.5)
        q2 = jnp.concatenate([jnp.where(lo, q, 0.0), jnp.where(lo, 0.0, q)], axis=0)
        scores[s, t] = _dot_nt(q2, kcat[kv_of_q[t]][keys(s)])
    probs, dens, lses = {}, {}, {}
    for s, t in items:
        sc = jnp.where(band_first if s == 0 else band, scores[s, t], NEG)
        m = jnp.max(sc, axis=-1, keepdims=True)
        if has_sink:
            c = t * LANES
            sk = jnp.where(first_head, sink_ref[0:1, c:c + 1], sink_ref[0:1, c + HEAD_DIM:c + HEAD_DIM + 1])
            m = jnp.maximum(m, sk)
        p = jnp.exp(sc - m)
        den = jnp.sum(p, axis=-1, keepdims=True)
        if has_sink:
            den = den + jnp.exp(sk - m)
        probs[s, t], dens[s, t] = p, den
        if emit_lse:
            lses[s, t] = m + jnp.log(den)
    for s, t in items:
        o2 = _dot(probs[s, t], vcat[kv_of_q[t]][keys(s)]) / dens[s, t]
        o_ref[rows(s), lanes(t)] = jnp.where(lo, o2[:BAND], o2[BAND:])
        if emit_lse:
            l2 = lses[s, t]
            lse_ref[rows(s), lanes(t)] = jnp.where(lo, l2[:BAND], l2[BAND:])


def band_attn(q, k, v, *, n, d, kv_of_q, sink=None, emit_lse=False):
    t, wq = q.shape
    wkv = k.shape[1]
    nq = wq // LANES
    nb = t // n // d // BAND
    qblocks = min(ATTN_QBLOCKS, nb)
    steps = nb // qblocks
    qv = q.reshape(t // d, d * wq)
    kv = k.reshape(t // d, d * wkv)
    vv = v.reshape(t // d, d * wkv)
    cur = lambda b, r, i: (b * steps + i, r)
    prev = lambda b, r, i: (b * nb + jnp.maximum(i * qblocks - 1, 0), r)
    in_specs = [pl.BlockSpec((qblocks * BAND, wq), cur),
                pl.BlockSpec((BAND, wkv), prev), pl.BlockSpec((qblocks * BAND, wkv), cur),
                pl.BlockSpec((BAND, wkv), prev), pl.BlockSpec((qblocks * BAND, wkv), cur)]
    args = [qv, kv, kv, vv, vv]
    if sink is not None:
        in_specs.append(pl.BlockSpec((1, wq), lambda b, r, i: (0, 0)))
        args.append(sink)
    n_out = 2 if emit_lse else 1
    outs = pl.pallas_call(
        functools.partial(_band_attn_kernel, nq=nq, kv_of_q=kv_of_q, has_sink=sink is not None,
                          emit_lse=emit_lse, qblocks=qblocks),
        grid=(n, d, steps),
        in_specs=in_specs,
        out_specs=[pl.BlockSpec((qblocks * BAND, wq), cur)] * n_out,
        out_shape=[jax.ShapeDtypeStruct((t // d, d * wq), F32)] * n_out,
        compiler_params=_cparams(("parallel", "parallel", "arbitrary")),
        name="band_attn_d%d" % d,
    )(*args)
    return [o.reshape(t, wq) for o in outs]


B_TILE = BAND * B_DILATIONS[-1]
B_WINDOW = ATTN_QBLOCKS * BAND


def _attend_items(items):
    lo = _iota((BAND, LANES), 1) < HEAD_DIM
    scores = []
    for q, k, _, _ in items:
        q2 = jnp.concatenate([jnp.where(lo, q, 0.0), jnp.where(lo, 0.0, q)], axis=0)
        scores.append(_dot_nt(q2, k))
    soft = []
    for (_, _, _, mask), sc in zip(items, scores):
        sc = jnp.where(mask, sc, NEG)
        m = jnp.max(sc, axis=-1, keepdims=True)
        p = jnp.exp(sc - m)
        den = jnp.sum(p, axis=-1, keepdims=True)
        soft.append((p, den, m + jnp.log(den)))
    outs = []
    for (_, _, v, _), (p, den, lse) in zip(items, soft):
        o2 = _dot(p, v) / den
        outs.append((jnp.where(lo, o2[:BAND], o2[BAND:]), jnp.where(lo, lse[:BAND], lse[BAND:])))
    return outs


def _merge_lse(o_old, l_old, o, l):
    m = jnp.maximum(l_old, l)
    w_old, w = jnp.exp(l_old - m), jnp.exp(l - m)
    den = w_old + w
    return (o_old * w_old + o * w) / den, m + jnp.log(den)


def _dilated_attn_kernel(q_ref, k_ref, v_ref, o_ref, kcat_ref, vcat_ref, qs_ref, os_ref, lse_ref,
                         qwin_ref, kwin_ref, vwin_ref, owin_ref, lwin_ref):
    i = pl.program_id(1)
    tiles = range(q_ref.shape[1] // LANES)
    lanes = lambda t: slice(t * LANES, (t + 1) * LANES)
    d1, d4, d16 = B_DILATIONS

    @pl.when(i == 0)
    def _():
        kcat_ref[:, :B_TILE] = jnp.zeros((len(tiles), B_TILE, LANES), F32)
        vcat_ref[:, :B_TILE] = jnp.zeros((len(tiles), B_TILE, LANES), F32)

    @pl.when(i > 0)
    def _():
        kcat_ref[:, :B_TILE] = kcat_ref[:, B_TILE:]
        vcat_ref[:, :B_TILE] = vcat_ref[:, B_TILE:]

    for t in tiles:
        kcat_ref[t, B_TILE:] = k_ref[:, lanes(t)]
        vcat_ref[t, B_TILE:] = v_ref[:, lanes(t)]
        qs_ref[t] = q_ref[:, lanes(t)] * (HEAD_DIM ** -0.5)

    qi = _iota((2 * BAND, 2 * BAND), 0) % BAND
    kj = _iota((2 * BAND, 2 * BAND), 1)
    band = (kj >= qi) & (kj <= qi + BAND)
    current = kj >= BAND

    def body1(w, carry):
        base = pl.multiple_of(w * B_WINDOW, B_WINDOW)
        back = pl.ds(pl.multiple_of(B_TILE - BAND + base, BAND), B_WINDOW + BAND)
        first = band & (current | (i > 0) | (w > 0))
        items = []
        for t in tiles:
            qw = qs_ref[t, pl.ds(base, B_WINDOW), :]
            kw = kcat_ref[t, back, :].astype(BF16)
            vw = vcat_ref[t, back, :].astype(BF16)
            items += [(qw[s * BAND:(s + 1) * BAND], kw[s * BAND:(s + 2) * BAND], vw[s * BAND:(s + 2) * BAND],
                       first if s == 0 else band) for s in range(ATTN_QBLOCKS)]
        for n, (o, lse) in enumerate(_attend_items(items)):
            t, s = divmod(n, ATTN_QBLOCKS)
            rows = pl.ds(pl.multiple_of(base + s * BAND, BAND), BAND)
            os_ref[t, rows, :] = o
            lse_ref[t, rows, :] = lse
        return carry

    lax.fori_loop(0, B_TILE // B_WINDOW, body1, 0)

    def body4(w, carry):
        span = BAND * d4
        base = pl.multiple_of(w * span, span)
        back = pl.ds(pl.multiple_of(B_TILE - span + base, span), 2 * span)
        qwin_ref[...] = qs_ref[:, pl.ds(base, span), :]
        kwin_ref[...] = kcat_ref[:, back, :]
        vwin_ref[...] = vcat_ref[:, back, :]
        owin_ref[...] = os_ref[:, pl.ds(base, span), :]
        lwin_ref[...] = lse_ref[:, pl.ds(base, span), :]
        mask = band & (current | (i > 0) | (w > 0))
        items = [(qwin_ref[t, pl.ds(r, BAND, stride=d4), :],
                  kwin_ref[t, pl.ds(r, 2 * BAND, stride=d4), :].astype(BF16),
                  vwin_ref[t, pl.ds(r, 2 * BAND, stride=d4), :].astype(BF16), mask) for t in tiles for r in range(d4)]
        for n, (o, lse) in enumerate(_attend_items(items)):
            t, r = divmod(n, d4)
            rows = pl.ds(r, BAND, stride=d4)
            o_new, l_new = _merge_lse(owin_ref[t, rows, :], lwin_ref[t, rows, :], o, lse)
            owin_ref[t, rows, :] = o_new
            lwin_ref[t, rows, :] = l_new
        os_ref[:, pl.ds(base, span), :] = owin_ref[...]
        lse_ref[:, pl.ds(base, span), :] = lwin_ref[...]
        return carry

    lax.fori_loop(0, B_TILE // (BAND * d4), body4, 0)

    mask = band & (current | (i > 0))
    for g in range(d16 // ATTN_QBLOCKS):
        classes = range(g * ATTN_QBLOCKS, (g + 1) * ATTN_QBLOCKS)
        items = [(qs_ref[t, pl.ds(r, BAND, stride=d16), :],
                  kcat_ref[t, pl.ds(r, 2 * BAND, stride=d16), :].astype(BF16),
                  vcat_ref[t, pl.ds(r, 2 * BAND, stride=d16), :].astype(BF16), mask) for t in tiles for r in classes]
        for n, (o, lse) in enumerate(_attend_items(items)):
            t, c = divmod(n, ATTN_QBLOCKS)
            rows = pl.ds(classes[c], BAND, stride=d16)
            os_ref[t, rows, :] = _merge_lse(os_ref[t, rows, :], lse_ref[t, rows, :], o, lse)[0]

    for t in tiles:
        o_ref[:, lanes(t)] = os_ref[t]


def dilated_attn(q, k, v, n):
    t, w = q.shape
    steps = t // n // B_TILE
    tile = pl.BlockSpec((B_TILE, w), lambda b, i: (b * steps + i, 0))
    tile_in = pl.BlockSpec((B_TILE, w), lambda b, i: (b * steps + i, 0), pipeline_mode=pl.Buffered(1))
    span4 = BAND * B_DILATIONS[1]
    nt = w // LANES
    return pl.pallas_call(
        _dilated_attn_kernel,
        grid=(n, steps),
        in_specs=[tile_in, tile_in, tile_in],
        out_specs=tile,
        out_shape=jax.ShapeDtypeStruct((t, w), F32),
        scratch_shapes=[pltpu.VMEM((nt, rows, LANES), F32) for rows in
                        (2 * B_TILE, 2 * B_TILE, B_TILE, B_TILE, B_TILE, span4, 2 * span4, 2 * span4, span4, span4)],
        compiler_params=_cparams(("parallel", "arbitrary")),
        name="dilated_attn",
    )(q, k, v)


NEW_PAD = 8


def _softmax_t(s_c, s_n, valid_c, valid_n, sink):
    s_c = jnp.where(valid_c, s_c, NEG)
    s_n = jnp.where(valid_n, s_n, NEG)
    m = jnp.maximum(jnp.max(s_c, axis=-1, keepdims=True), jnp.max(s_n, axis=-1, keepdims=True))
    if sink is not None:
        m = jnp.maximum(m, sink)
    p_c = jnp.exp(s_c - m)
    p_n = jnp.exp(s_n - m)
    den = jnp.sum(p_c, axis=-1, keepdims=True) + jnp.sum(p_n, axis=-1, keepdims=True)
    if sink is not None:
        den = den + jnp.exp(sink - m)
    return p_c, p_n, den, m + jnp.log(den)


def _sample_attn_kernel(qa_ref, kan_ref, van_ref, cak_ref, cav_ref, sink_ref,
                        qb_ref, kbn_ref, vbn_ref, cbk_ref, cbv_ref, oa_ref, ob_ref, *, n_new):
    pa = cak_ref.shape[-1]
    pb = cbk_ref.shape[-1]
    groups_a, heads_b = cak_ref.shape[2], cbk_ref.shape[2]
    scale = HEAD_DIM ** -0.5
    new_i = _iota((NEW_PAD, LANES), 1)
    live_n = new_i < n_new
    jn = _iota((NEW_PAD, LANES), 0)

    qa = [(qa_ref[0, g] * scale).astype(BF16) for g in range(groups_a)]
    qb = [(qb_ref[0, h] * scale).astype(BF16) for h in range(heads_b)]
    sa_c = [_dot(qa[g], cak_ref[0, 0, g]) for g in range(groups_a)]
    sa_n = [_dot(qa[g], kan_ref[0, g]) for g in range(groups_a)]
    sb_c = [_dot(qb[h], cbk_ref[0, 0, h]) for h in range(heads_b)]
    sb_n = [_dot(qb[h], kbn_ref[0, h]) for h in range(heads_b)]

    valid_c = _iota((NEW_PAD, pa), 1) >= _iota((NEW_PAD, pa), 0) % n_new
    valid_n = live_n & (new_i <= jn % n_new)
    soft_a = [_softmax_t(sa_c[g], sa_n[g], valid_c, valid_n, sink_ref[g][:, 0:1]) for g in range(groups_a)]
    soft_b = {}
    for d in B_DILATIONS:
        span = BAND * d
        jc = _iota((NEW_PAD, span), 0)
        pc = _iota((NEW_PAD, span), 1)
        if d == 1:
            valid_c, valid_n = pc >= jc, live_n & (new_i <= jn)
        else:
            valid_c, valid_n = pc % d == jc, live_n & (new_i == jn)
        for h in range(heads_b):
            soft_b[h, d] = _softmax_t(sb_c[h][:, pb - span:], sb_n[h], valid_c, valid_n, None)

    for g in range(groups_a):
        p_c, p_n, den, _ = soft_a[g]
        oa_ref[0, g] = (_dot_nt(p_c, cav_ref[0, 0, g]) + _dot_nt(p_n, van_ref[0, g])) / den
    outs = {}
    for h in range(heads_b):
        for d in B_DILATIONS:
            p_c, p_n, den, _ = soft_b[h, d]
            outs[h, d] = (_dot_nt(p_c, cbv_ref[0, 0, h, :, pb - BAND * d:]) + _dot_nt(p_n, vbn_ref[0, h])) / den
    for h in range(heads_b):
        lses = [soft_b[h, d][3] for d in B_DILATIONS]
        m = jnp.maximum(jnp.maximum(lses[0], lses[1]), lses[2])
        ws = [jnp.exp(l - m) for l in lses]
        o1, o2, o3 = [outs[h, d] for d in B_DILATIONS]
        ob_ref[0, h] = (ws[0] * o1 + ws[1] * o2 + ws[2] * o3) / (ws[0] + ws[1] + ws[2])


def _seg_ones(w, dtype):
    r = jnp.arange(w) // HEAD_DIM
    return (r[:, None] == r[None, :]).astype(dtype)


def _new_keys_t(x, n, l, heads):
    xt = x.reshape(n, l, heads, HEAD_DIM).transpose(0, 2, 3, 1)
    return jnp.pad(xt, ((0, 0), (0, 0), (0, 0), (0, LANES - l)))


def sample_attn(qa, ka, va, cache_a_k_t, cache_a_v_t, sink, qb, kb, vb, cache_b_k_t, cache_b_v_t, layer, n, l):
    ga, gb = cache_a_k_t.shape[2], cache_b_k_t.shape[2]
    pa, pb = cache_a_k_t.shape[4], cache_b_k_t.shape[4]
    reps = A_Q // A_KV
    assert pa == BAND and pb == BAND * B_DILATIONS[-1] and l * reps == NEW_PAD and l <= B_DILATIONS[1]
    qa_g = qa.reshape(n, l, reps, ga, HEAD_DIM).transpose(0, 3, 2, 1, 4).reshape(n, ga, NEW_PAD, HEAD_DIM)
    sink_g = jnp.broadcast_to(sink.reshape(ga, reps, 1, 1), (ga, reps, l, LANES)).reshape(ga, NEW_PAD, LANES)
    qb_g = jnp.pad(qb.reshape(n, l, gb, HEAD_DIM).transpose(0, 2, 1, 3), ((0, 0), (0, 0), (0, NEW_PAD - l), (0, 0)))
    rows = lambda g: pl.BlockSpec((1, g, NEW_PAD, HEAD_DIM), lambda b: (b, 0, 0, 0))
    cols = lambda g: pl.BlockSpec((1, g, HEAD_DIM, LANES), lambda b: (b, 0, 0, 0))
    cache = lambda g, p: pl.BlockSpec((1, 1, g, HEAD_DIM, p), lambda b: (layer, b, 0, 0, 0))
    oa, ob = pl.pallas_call(
        functools.partial(_sample_attn_kernel, n_new=l),
        grid=(n,),
        in_specs=[rows(ga), cols(ga), cols(ga), cache(ga, pa), cache(ga, pa),
                  pl.BlockSpec((ga, NEW_PAD, LANES), lambda b: (0, 0, 0)),
                  rows(gb), cols(gb), cols(gb), cache(gb, pb), cache(gb, pb)],
        out_specs=[rows(ga), rows(gb)],
        out_shape=[jax.ShapeDtypeStruct((n, ga, NEW_PAD, HEAD_DIM), F32),
                   jax.ShapeDtypeStruct((n, gb, NEW_PAD, HEAD_DIM), F32)],
        compiler_params=_cparams(("parallel",)),
        name="sample_attn",
    )(qa_g, _new_keys_t(ka, n, l, ga), _new_keys_t(va, n, l, ga), cache_a_k_t, cache_a_v_t, sink_g,
      qb_g, _new_keys_t(kb, n, l, gb), _new_keys_t(vb, n, l, gb), cache_b_k_t, cache_b_v_t)
    oa = oa.reshape(n, ga, reps, l, HEAD_DIM).transpose(0, 3, 2, 1, 4).reshape(n * l, A_Q)
    ob = ob[:, :, :l].transpose(0, 2, 1, 3).reshape(n * l, B_W)
    return oa, ob


def _rwkv_prep_kernel(z_ref, zfirst_ref, mu_ref, w0_ref, a0_ref, kk_ref, ka_ref, w2_ref, a2_ref, g2_ref, seg_ref,
                      r_out, k_out, v_out, al_out, b_out, lw_out, g_out, carry_ref, *, seq_len):
    tile = pl.program_id(1)
    z = z_ref[...]
    tm = z.shape[0]
    rolled = pltpu.roll(z, 1, axis=0)
    if seq_len >= tm:
        @pl.when(tile == 0)
        def _():
            carry_ref[...] = zfirst_ref[0]

        prev = jnp.where(_iota((tm, 1), 0) == 0, carry_ref[...], rolled)
        carry_ref[...] = z[tm - 1:tm, :]
    else:
        prev = jnp.where(_iota((tm, 1), 0) % seq_len == 0, zfirst_ref[...], rolled)
    zs = z + (prev - z) * mu_ref[...]
    r = zs[:, 0:C_W]
    k = zs[:, C_W:2 * C_W]
    v = zs[:, 2 * C_W:3 * C_W]
    lora = zs[:, 3 * C_W:3 * C_W + LANES]
    zg = zs[:, 3 * C_W + LANES:]
    w_log = -jax.nn.softplus(-(w0_ref[...] + _dot_x3(jnp.tanh(lora), w2_ref[...]))) - 0.5
    a = jax.nn.sigmoid(a0_ref[...] + _dot_x3(lora, a2_ref[...]))
    g = _dot_x3(jax.nn.sigmoid(zg), g2_ref[...])
    kk = k * kk_ref[...]
    kk = kk / jnp.maximum(jnp.sqrt(_dot_x2(kk * kk, seg_ref[...])), 1e-12)
    r_out[...] = r
    k_out[...] = k * (1.0 + (a - 1.0) * ka_ref[...])
    v_out[...] = v
    al_out[...] = -kk
    b_out[...] = kk * a
    lw_out[...] = -jnp.exp(w_log)
    g_out[...] = g


def rwkv_prep(zc, z_before, lw, n_seq):
    t = zc.shape[0]
    seq_len = t // n_seq
    tm = min(PREP_ROWS, t)
    if seq_len >= tm:
        grid = (n_seq, seq_len // tm)
        zfirst = z_before.reshape(n_seq, 1, C_PROJ)
        zfirst_spec = pl.BlockSpec((1, 1, C_PROJ), lambda b, i: (b, 0, 0))
    else:
        grid = (1, t // tm)
        zfirst = jnp.pad(z_before.reshape(n_seq, 1, C_PROJ), ((0, 0), (0, seq_len - 1), (0, 0))).reshape(t, C_PROJ)
        zfirst_spec = pl.BlockSpec((tm, C_PROJ), lambda b, i: (i, 0))
    tiles = grid[1]
    row = lambda b, i: (b * tiles + i, 0)
    const = lambda s: pl.BlockSpec(s, lambda b, i: (0, 0))
    vec = lambda a: a.reshape(1, -1)
    zero64 = jnp.zeros((HEAD_DIM, C_W), F32)
    w2p = jnp.concatenate([lw['c_w2'], zero64], axis=0)
    a2p = jnp.concatenate([zero64, lw['c_a2']], axis=0)
    out_spec = pl.BlockSpec((tm, C_W), row)
    return pl.pallas_call(
        functools.partial(_rwkv_prep_kernel, seq_len=seq_len),
        grid=grid,
        in_specs=[pl.BlockSpec((tm, C_PROJ), row),
                  zfirst_spec,
                  const((1, C_PROJ)), const((1, C_W)), const((1, C_W)), const((1, C_W)), const((1, C_W)),
                  const((LANES, C_W)), const((LANES, C_W)), const((LANES, C_W)), const((C_W, C_W))],
        out_specs=[out_spec] * 7,
        out_shape=[jax.ShapeDtypeStruct((t, C_W), F32)] * 7,
        scratch_shapes=[pltpu.VMEM((1, C_PROJ), F32)],
        compiler_params=_cparams(("parallel", "arbitrary")),
        name="rwkv_prep",
    )(zc, zfirst, vec(lw['c_mu']), vec(lw['c_w0']), vec(lw['c_a0']), vec(lw['c_k_k']), vec(lw['c_k_a']),
      w2p, a2p, lw['c_g2'], _seg_ones(C_W, F32))


SCAN_GROUP = 8
SCAN_PACKED_GROUP = 2


def _pair_block_diag(x, mask_b):
    xb = x.astype(BF16)
    return jnp.concatenate([xb, xb], axis=0) * mask_b


def _rwkv_scan_kernel(r_ref, k_ref, v_ref, al_ref, b_ref, lw_ref, g_ref, s0_ref, rk_ref, lnw_ref, lnb_ref, seg_ref,
                      o_ref, sfin_ref, s_ref, *, seq_len, group):
    step = pl.program_id(1)
    pairs = C_W // LANES
    rows = group * CHUNK
    chain = seq_len >= CHUNK
    span = min(seq_len, CHUNK)
    per = CHUNK // span
    ri = _iota((LANES, LANES), 0)
    ci = _iota((LANES, LANES), 1)
    bd_mask = (ri // HEAD_DIM) == (ci // HEAD_DIM)
    mask_b = jnp.where(bd_mask, 1.0, 0.0).astype(BF16)
    bd = lambda x: _pair_block_diag(x, mask_b)

    if chain:
        @pl.when(step == 0)
        def _():
            s_ref[...] = s0_ref[0]

    r, k, v, al, b, lw = r_ref[...], k_ref[...], v_ref[...], al_ref[...], b_ref[...], lw_ref[...]

    ti = _iota((rows, rows), 0)
    si = _iota((rows, rows), 1)
    same = ti // span == si // span
    tri = jnp.where((si <= ti) & same, 1.0, 0.0).astype(BF16)
    ones = jnp.where(same, 1.0, 0.0).astype(BF16)
    lw_hi, lw_lo = _split(lw)
    cum = jnp.dot(tri, lw_hi, preferred_element_type=F32) + jnp.dot(tri, lw_lo, preferred_element_type=F32)
    cum_end = jnp.dot(ones, lw_hi, preferred_element_type=F32) + jnp.dot(ones, lw_lo, preferred_element_type=F32)
    e_neg = jnp.exp(-cum)
    e_rem = jnp.exp(cum_end - cum)
    a_t = al * jnp.exp(cum - lw)
    b_t = b * e_neg
    k_t = k * e_neg
    r_t = r * jnp.exp(cum)
    b_h = b * e_rem
    k_h = k * e_rem

    tt = _iota((CHUNK, LANES), 0)
    ss = _iota((CHUNK, LANES), 1) % CHUNK
    together = tt // span == ss // span
    strict = (tt > ss) & together
    incl = (tt >= ss) & together
    eye = (tt == ss).astype(F32)
    items = [(c, m) for c in range(group) for m in range(pairs)]
    tile = lambda x, it: x[it[0] * CHUNK:(it[0] + 1) * CHUNK, it[1] * LANES:(it[1] + 1) * LANES]

    gram = {}
    for it in items:
        ar = jnp.concatenate([tile(a_t, it), tile(r_t, it)], axis=0)
        bk = jnp.concatenate([bd(tile(b_t, it)), bd(tile(k_t, it))], axis=0)
        gram[it] = _dot_nt(ar, bk)
    m_ab = {it: jnp.where(strict, gram[it][:CHUNK, :LANES], 0.0) for it in items}
    n_rb = {it: jnp.where(incl, gram[it][CHUNK:, :LANES], 0.0) for it in items}
    m_ak = {it: jnp.where(strict, gram[it][:CHUNK, LANES:], 0.0) for it in items}
    n_rk = {it: jnp.where(incl, gram[it][CHUNK:, LANES:], 0.0) for it in items}
    t_inv = {it: eye + jnp.where(tt // 2 == ss // 2, m_ab[it], 0.0) for it in items}
    blk = 4
    while blk <= span:
        joins = (tt // blk == ss // blk) & (tt // (blk // 2) != ss // (blk // 2))
        de = {it: _dot(t_inv[it], bd(jnp.where(joins, m_ab[it], 0.0))) for it in items}
        t_inv = {it: t_inv[it] + _dot(de[it], bd(t_inv[it])) for it in items}
        blk *= 2
    wv = {it: _dot(jnp.concatenate([m_ak[it], n_rk[it]], axis=0), bd(tile(v, it))) for it in items}
    pq = {it: _dot(t_inv[it], jnp.concatenate([bd(tile(a_t, it)), bd(wv[it][:CHUNK])], axis=1)) for it in items}

    y = {}
    if chain:
        state = [s_ref[m] for m in range(pairs)]
        for c in range(group):
            for m in range(pairs):
                it = (c, m)
                uy = _dot_nt(jnp.concatenate([pq[it][:, :LANES], tile(r_t, it)], axis=0), state[m])
                u = uy[:CHUNK] + pq[it][:, LANES:]
                y[it] = uy[CHUNK:] + _dot(n_rb[it], bd(u)) + wv[it][CHUNK:]
                upd = _dot(jnp.concatenate([u, tile(v, it)], axis=0).T,
                           jnp.concatenate([tile(b_h, it), tile(k_h, it)], axis=0))
                g_end = jnp.exp(tile(cum_end, it)[0:1, :])
                state[m] = state[m] * g_end + jnp.where(bd_mask, upd, 0.0)
        for m in range(pairs):
            s_ref[m] = state[m]
    else:
        row_seq = _iota((2 * CHUNK, 1), 0) % CHUNK // span
        col_seq = _iota((LANES, 2 * CHUNK), 1) % CHUNK // span
        for it in items:
            c, m = it
            pr = jnp.concatenate([pq[it][:, :LANES], tile(r_t, it)], axis=0)
            uy = None
            for j in range(per):
                part = _dot_nt(jnp.where(row_seq == j, pr, 0.0), s0_ref[c * per + j, m])
                uy = part if uy is None else uy + part
            u = uy[:CHUNK] + pq[it][:, LANES:]
            y[it] = uy[CHUNK:] + _dot(n_rb[it], bd(u)) + wv[it][CHUNK:]
            uv_t = jnp.concatenate([u, tile(v, it)], axis=0).T
            bk = jnp.concatenate([tile(b_h, it), tile(k_h, it)], axis=0).astype(BF16)
            for j in range(per):
                upd = _dot(jnp.where(col_seq == j, uv_t, 0.0), bk)
                g_end = jnp.exp(tile(cum_end, it)[j * span:j * span + 1, :])
                sfin_ref[c * per + j, m] = s0_ref[c * per + j, m] * g_end + jnp.where(bd_mask, upd, 0.0)

    seg = seg_ref[...]
    for m in range(pairs):
        sl = slice(m * LANES, (m + 1) * LANES)
        ym = jnp.concatenate([y[(c, m)] for c in range(group)], axis=0)
        mu = _dot_x2(ym, seg) * (1.0 / HEAD_DIM)
        dy = ym - mu
        var = _dot_x2(dy * dy, seg) * (1.0 / HEAD_DIM)
        yn = dy * lax.rsqrt(var + GN_EPS) * lnw_ref[:, sl] + lnb_ref[:, sl]
        bonus = _dot_x2(r[:, sl] * k[:, sl] * rk_ref[:, sl], seg)
        o_ref[:, sl] = (yn + bonus * v[:, sl]) * g_ref[:, sl]

    if chain:
        @pl.when(step == pl.num_programs(1) - 1)
        def _():
            sfin_ref[0] = s_ref[...]


def _state_to_pairs(s):
    n, h, hd, _ = s.shape
    s = s.reshape(n, h // 2, 2, hd, hd)
    bd = s[:, :, :, :, None, :] * jnp.eye(2, dtype=s.dtype)[None, None, :, None, :, None]
    return bd.reshape(n, h // 2, 2 * hd, 2 * hd)


def _pairs_to_state(bd):
    n, p, w, _ = bd.shape
    hd = w // 2
    bd = bd.reshape(n, p, 2, hd, 2, hd)
    return jnp.stack([bd[:, :, 0, :, 0, :], bd[:, :, 1, :, 1, :]], axis=2).reshape(n, 2 * p, hd, hd)


def rwkv_scan(prep, s0, lw, n_seq):
    t = prep[0].shape[0]
    seq_len = t // n_seq
    pairs = C_W // LANES
    if seq_len >= CHUNK:
        group = min(SCAN_GROUP, seq_len // CHUNK)
        grid = (n_seq, seq_len // CHUNK // group)
        n_state = 1
    else:
        group = min(SCAN_PACKED_GROUP, t // CHUNK)
        grid = (t // CHUNK // group, 1)
        n_state = group * (CHUNK // seq_len)
    steps = grid[1]
    row = pl.BlockSpec((group * CHUNK, C_W), lambda b, c: (b * steps + c, 0))
    const = lambda s: pl.BlockSpec(s, lambda b, c: (0, 0))
    state = pl.BlockSpec((n_state, pairs, LANES, LANES), lambda b, c: (b, 0, 0, 0))
    vec = lambda a: a.reshape(1, -1)
    out, s_fin = pl.pallas_call(
        functools.partial(_rwkv_scan_kernel, seq_len=seq_len, group=group),
        grid=grid,
        in_specs=[row] * 7 + [state, const((1, C_W)), const((1, C_W)), const((1, C_W)), const((LANES, LANES))],
        out_specs=[row, state],
        out_shape=[jax.ShapeDtypeStruct((t, C_W), F32), jax.ShapeDtypeStruct((n_seq, pairs, LANES, LANES), F32)],
        scratch_shapes=[pltpu.VMEM((pairs, LANES, LANES), F32)],
        compiler_params=_cparams(("parallel", "arbitrary")),
        name="rwkv_scan",
    )(*prep, _state_to_pairs(s0), vec(lw['c_r_k']), vec(lw['c_ln_w']), vec(lw['c_ln_b']), _seg_ones(LANES, F32))
    return out, _pairs_to_state(s_fin)


def _out_proj_kernel(x_ref, oa_ref, ob_ref, oc_ref, w_ref, y_ref):
    acc = x_ref[...]
    acc += jnp.dot(oa_ref[...].astype(BF16), w_ref[0:A_Q, :], preferred_element_type=F32)
    acc += jnp.dot(ob_ref[...].astype(BF16), w_ref[A_Q:A_Q + B_W, :], preferred_element_type=F32)
    acc += jnp.dot(oc_ref[...].astype(BF16), w_ref[A_Q + B_W:, :], preferred_element_type=F32)
    y_ref[...] = acc


def out_proj(x, oa, ob, oc, w_bf16, tm):
    t, d = x.shape
    row = lambda w: pl.BlockSpec((tm, w), lambda i: (i, 0))
    return pl.pallas_call(
        _out_proj_kernel, grid=(t // tm,),
        in_specs=[row(d), row(A_Q), row(B_W), row(C_W), pl.BlockSpec(w_bf16.shape, lambda i: (0, 0))],
        out_specs=row(d), out_shape=jax.ShapeDtypeStruct((t, d), F32),
        compiler_params=_cparams(("parallel",)), name="out_proj",
    )(x, oa, ob, oc, w_bf16)


FF_CHUNK = 1024


def _mlp_kernel(x_ref, g_ref, up_ref, down_ref, y_ref):
    x = x_ref[...]
    h = (x * lax.rsqrt(jnp.mean(x * x, axis=-1, keepdims=True) + RMS_EPS) * g_ref[...]).astype(BF16)
    acc = x
    for c in range(up_ref.shape[1] // FF_CHUNK):
        u = jnp.dot(h, up_ref[:, c * FF_CHUNK:(c + 1) * FF_CHUNK], preferred_element_type=F32)
        u = jnp.square(jnp.maximum(u, 0.0)).astype(BF16)
        acc += jnp.dot(u, down_ref[c * FF_CHUNK:(c + 1) * FF_CHUNK, :], preferred_element_type=F32)
    y_ref[...] = acc


def mlp(x, g, up_bf16, down_bf16, tm):
    t, d = x.shape
    row = pl.BlockSpec((tm, d), lambda i: (i, 0))
    return pl.pallas_call(
        _mlp_kernel, grid=(t // tm,),
        in_specs=[row, pl.BlockSpec((1, d), lambda i: (0, 0)),
                  pl.BlockSpec(up_bf16.shape, lambda i: (0, 0), pipeline_mode=pl.Buffered(1)),
                  pl.BlockSpec(down_bf16.shape, lambda i: (0, 0), pipeline_mode=pl.Buffered(1))],
        out_specs=row, out_shape=jax.ShapeDtypeStruct((t, d), F32),
        compiler_params=_cparams(("parallel",)), name="mlp",
    )(x, g.reshape(1, d), up_bf16, down_bf16)


def _final_norm_kernel(x_ref, g_ref, y_ref):
    x = x_ref[...]
    y_ref[...] = x * lax.rsqrt(jnp.mean(x * x, axis=-1, keepdims=True) + RMS_EPS) * g_ref[...]


def final_norm(x, g, tm):
    t, d = x.shape
    row = pl.BlockSpec((tm, d), lambda i: (i, 0))
    return pl.pallas_call(
        _final_norm_kernel, grid=(t // tm,),
        in_specs=[row, pl.BlockSpec((1, d), lambda i: (0, 0))],
        out_specs=row, out_shape=jax.ShapeDtypeStruct((t, d), F32),
        compiler_params=_cparams(("parallel",)), name="final_norm",
    )(x, g.reshape(1, d))


PREP_ROWS = 256
A_PERM = (0, 2, 1, 3)


def _permute_a_heads(w, axis):
    take = lambda a, b: lax.slice_in_dim(w, a, b, axis=axis)
    parts = [take(h * HEAD_DIM, (h + 1) * HEAD_DIM) for h in A_PERM] + [take(A_Q, w.shape[axis])]
    return jnp.concatenate(parts, axis=axis)


def _row_tile(t):
    return 512 if t % 512 == 0 else t


def trunk_layer(x, lw, n, past, layer):
    t = x.shape[0]
    l = t // n
    tm = _row_tile(t)
    qa, ka, va, qb, kb, vb, zc = norm_proj(x, lw['norm1_g'], lw['w_in'], tm)
    if past is None:
        oa, = band_attn(qa, ka, va, n=n, d=1, kv_of_q=(0, 0), sink=lw['sink'])
        ob = dilated_attn(qb, kb, vb, n)
        z_before = jnp.zeros((n, C_PROJ), F32)
        s0 = jnp.zeros((n, C_W // HEAD_DIM, HEAD_DIM, HEAD_DIM), F32)
    else:
        cache_a_k_t, cache_a_v_t, cache_b_k_t, cache_b_v_t, state_wkv, state_shift = past
        oa, ob = sample_attn(qa, ka, va, cache_a_k_t, cache_a_v_t, lw['a_sink'],
                             qb, kb, vb, cache_b_k_t, cache_b_v_t, layer, n, l)
        z_before, s0 = state_shift[layer], state_wkv[layer]
    oc, s_fin = rwkv_scan(rwkv_prep(zc, z_before, lw, n), s0, lw, n)
    x = out_proj(x, oa, ob, oc, lw['w_out'], tm)
    x = mlp(x, lw['norm2_g'], lw['mlp_up'], lw['mlp_down'], tm)
    heads = lambda a, h: a.reshape(n, l, h, HEAD_DIM)
    ra, rb = min(BAND, l), min(BAND * B_DILATIONS[-1], l)
    new_state = (heads(ka, 2)[:, l - ra:], heads(va, 2)[:, l - ra:], heads(kb, 6)[:, l - rb:], heads(vb, 6)[:, l - rb:],
                 s_fin, zc.reshape(n, l, C_PROJ)[:, -1])
    return x, new_state


def kernel(x_prompt, x_sample, cache_a_k, cache_a_v, cache_b_k, cache_b_v, state_c_wkv, state_c_shift, norm1_g, norm2_g, w_in, w_out, a_sink, c_mu, c_w0, c_w2, c_a0, c_a2, c_g2, c_k_k, c_k_a, c_r_k, c_ln_w, c_ln_b, mlp_up, mlp_down, final_norm_g):
    depth = w_in.shape[0]
    n_p, l_p, d_model = x_prompt.shape
    n_s, l_s, _ = x_sample.shape
    w_in_b = _permute_a_heads(w_in, 2).astype(BF16)
    w_out_b = _permute_a_heads(w_out, 1).astype(BF16)
    up_b = mlp_up.astype(BF16)
    down_b = mlp_down.astype(BF16)
    sink = jnp.repeat(a_sink[:, jnp.array(A_PERM)], HEAD_DIM, axis=1).reshape(depth, 1, A_Q)
    cache_t = lambda c: jnp.transpose(c, (0, 1, 3, 4, 2))
    past = (cache_t(cache_a_k), cache_t(cache_a_v), cache_t(cache_b_k), cache_t(cache_b_v), state_c_wkv, state_c_shift)

    yp = x_prompt.reshape(n_p * l_p, d_model)
    ys = x_sample.reshape(n_s * l_s, d_model)
    p_new, s_new = [], []
    for l in range(depth):
        lw = {'norm1_g': norm1_g[l], 'norm2_g': norm2_g[l], 'w_in': w_in_b[l], 'w_out': w_out_b[l], 'sink': sink[l], 'a_sink': a_sink[l],
              'c_mu': c_mu[l], 'c_w0': c_w0[l], 'c_w2': c_w2[l], 'c_a0': c_a0[l], 'c_a2': c_a2[l], 'c_g2': c_g2[l],
              'c_k_k': c_k_k[l], 'c_k_a': c_k_a[l], 'c_r_k': c_r_k[l], 'c_ln_w': c_ln_w[l], 'c_ln_b': c_ln_b[l],
              'mlp_up': up_b[l], 'mlp_down': down_b[l]}
        yp, st_p = trunk_layer(yp, lw, n_p, None, l)
        ys, st_s = trunk_layer(ys, lw, n_s, past, l)
        p_new.append(st_p)
        s_new.append(st_s)
    p_state = [jnp.stack(t) for t in zip(*p_new)]
    s_state = [jnp.stack(t) for t in zip(*s_new)]
    y_prompt = final_norm(yp, final_norm_g, _row_tile(yp.shape[0])).reshape(n_p, l_p, d_model)
    y_sample = final_norm(ys, final_norm_g, _row_tile(ys.shape[0])).reshape(n_s, l_s, d_model)
    return (y_prompt, y_sample, *p_state, *s_state)
```

```python
import functools

import jax
import jax.numpy as jnp
from jax import lax
from jax.experimental import pallas as pl
from jax.experimental.pallas import tpu as pltpu

F32 = jnp.float32
BF16 = jnp.bfloat16

LANES = 128
HEAD_DIM = 64
BAND = 128
CHUNK = 64
RMS_EPS = 1e-5
GN_EPS = 64e-5
NEG = -1e30
VMEM_LIMIT = 56 * 1024 * 1024

A_Q, A_KV, B_W, C_W = 256, 128, 384, 384
C_PROJ = 1408
SPLITS = (0, 256, 384, 512, 896, 1280, 1664, 3072)
B_DILATIONS = (1, 4, 16)


def _cparams(sem):
    return pltpu.CompilerParams(dimension_semantics=sem, vmem_limit_bytes=VMEM_LIMIT)


def _dot(a, b):
    return jnp.dot(a.astype(BF16), b.astype(BF16), preferred_element_type=F32)


def _dot_nt(a, b):
    return lax.dot_general(a.astype(BF16), b.astype(BF16), (((1,), (1,)), ((), ())),
                           preferred_element_type=F32)


def _split(a):
    hi = a.astype(BF16)
    return hi, (a - hi.astype(F32)).astype(BF16)


def _dot_x2(a, b_exact):
    hi, lo = _split(a)
    b = b_exact.astype(BF16)
    return jnp.dot(hi, b, preferred_element_type=F32) + jnp.dot(lo, b, preferred_element_type=F32)


def _dot_x3(a, b):
    ah, al = _split(a)
    bh, bl = _split(b)
    return (jnp.dot(ah, bh, preferred_element_type=F32) + jnp.dot(al, bh, preferred_element_type=F32)
            + jnp.dot(ah, bl, preferred_element_type=F32))


def _iota(shape, dim):
    return lax.broadcasted_iota(jnp.int32, shape, dim)


KV_SPLITS = (1, 2, 4, 5)


def _norm_proj_kernel(x_ref, g_ref, w_ref, *out_refs, tail_a):
    x = x_ref[...]
    h = x * lax.rsqrt(jnp.mean(x * x, axis=-1, keepdims=True) + RMS_EPS) * g_ref[...]
    hb = h.astype(BF16)
    vals = []
    for o_ref, a, b in zip(out_refs, SPLITS[:-1], SPLITS[1:]):
        vals.append(jnp.dot(hb, w_ref[:, a:b], preferred_element_type=F32))
        o_ref[...] = vals[-1]
    if tail_a:
        tm = x.shape[0]
        ka_t, va_t, kb_t, vb_t = out_refs[len(vals):]
        ka_t[0] = vals[KV_SPLITS[0]][tm - tail_a:, :].T
        va_t[0] = vals[KV_SPLITS[1]][tm - tail_a:, :].T
        kb_t[0] = vals[KV_SPLITS[2]].T
        vb_t[0] = vals[KV_SPLITS[3]].T


def norm_proj(x, g, w_bf16, n, tails):
    t, d = x.shape
    l = t // n
    tm = _row_tile(l)
    tiles = l // tm
    widths = [b - a for a, b in zip(SPLITS[:-1], SPLITS[1:])]
    row = lambda b, i: (b * tiles + i, 0)
    out_specs = [pl.BlockSpec((tm, w), row) for w in widths]
    out_shape = [jax.ShapeDtypeStruct((t, w), F32) for w in widths]
    tail_a = min(BAND, l) if tails else 0
    if tails:
        tail_b = min(BAND * B_DILATIONS[-1], l)
        skip = (l - tail_b) // tm
        out_specs += [pl.BlockSpec((1, A_KV, tail_a), lambda b, i: (b, 0, 0))] * 2
        out_specs += [pl.BlockSpec((1, B_W, tm), lambda b, i: (b, 0, jnp.maximum(i - skip, 0)))] * 2
        out_shape += [jax.ShapeDtypeStruct((n, A_KV, tail_a), F32)] * 2
        out_shape += [jax.ShapeDtypeStruct((n, B_W, tail_b), F32)] * 2
    return pl.pallas_call(
        functools.partial(_norm_proj_kernel, tail_a=tail_a),
        grid=(n, tiles),
        in_specs=[pl.BlockSpec((tm, d), row),
                  pl.BlockSpec((1, d), lambda b, i: (0, 0)),
                  pl.BlockSpec(w_bf16.shape, lambda b, i: (0, 0))],
        out_specs=out_specs,
        out_shape=out_shape,
        compiler_params=_cparams(("parallel", "arbitrary")),
        name="norm_proj",
    )(x, g.reshape(1, d), w_bf16)


ATTN_QBLOCKS = 4


def _band_attn_kernel(q_ref, kp_ref, kc_ref, vp_ref, vc_ref, sink_ref, o_ref, *, nq, kv_of_q, qblocks):
    i = pl.program_id(1)
    qi = _iota((2 * BAND, 2 * BAND), 0) % BAND
    kj = _iota((2 * BAND, 2 * BAND), 1)
    band = (kj >= qi) & (kj <= qi + BAND)
    band_first = band & ((kj >= BAND) | (i > 0))
    first_head = _iota((2 * BAND, 1), 0) < BAND
    lo = _iota((BAND, LANES), 1) < HEAD_DIM
    nkv = kp_ref.shape[1] // LANES
    lanes = lambda t: slice(t * LANES, (t + 1) * LANES)
    kcat = [jnp.concatenate([kp_ref[:, lanes(t)], kc_ref[:, lanes(t)]], axis=0).astype(BF16) for t in range(nkv)]
    vcat = [jnp.concatenate([vp_ref[:, lanes(t)], vc_ref[:, lanes(t)]], axis=0).astype(BF16) for t in range(nkv)]
    items = [(s, t) for s in range(qblocks) for t in range(nq)]
    rows = lambda s: slice(s * BAND, (s + 1) * BAND)
    keys = lambda s: slice(s * BAND, (s + 2) * BAND)

    scores = {}
    for s, t in items:
        q = q_ref[rows(s), lanes(t)] * (HEAD_DIM ** -0.5)
        q2 = jnp.concatenate([jnp.where(lo, q, 0.0), jnp.where(lo, 0.0, q)], axis=0)
        scores[s, t] = _dot_nt(q2, kcat[kv_of_q[t]][keys(s)])
    probs, dens = {}, {}
    for s, t in items:
        sc = jnp.where(band_first if s == 0 else band, scores[s, t], NEG)
        c = t * LANES
        sk = jnp.where(first_head, sink_ref[0:1, c:c + 1], sink_ref[0:1, c + HEAD_DIM:c + HEAD_DIM + 1])
        m = jnp.maximum(jnp.max(sc, axis=-1, keepdims=True), sk)
        p = jnp.exp(sc - m)
        probs[s, t], dens[s, t] = p, jnp.sum(p, axis=-1, keepdims=True) + jnp.exp(sk - m)
    for s, t in items:
        o2 = _dot(probs[s, t], vcat[kv_of_q[t]][keys(s)]) / dens[s, t]
        o_ref[rows(s), lanes(t)] = jnp.where(lo, o2[:BAND], o2[BAND:])


def band_attn(q, k, v, sink, *, n, kv_of_q):
    t, wq = q.shape
    wkv = k.shape[1]
    nb = t // n // BAND
    qblocks = min(ATTN_QBLOCKS, nb)
    steps = nb // qblocks
    cur = lambda b, i: (b * steps + i, 0)
    prev = lambda b, i: (b * nb + jnp.maximum(i * qblocks - 1, 0), 0)
    return pl.pallas_call(
        functools.partial(_band_attn_kernel, nq=wq // LANES, kv_of_q=kv_of_q, qblocks=qblocks),
        grid=(n, steps),
        in_specs=[pl.BlockSpec((qblocks * BAND, wq), cur),
                  pl.BlockSpec((BAND, wkv), prev), pl.BlockSpec((qblocks * BAND, wkv), cur),
                  pl.BlockSpec((BAND, wkv), prev), pl.BlockSpec((qblocks * BAND, wkv), cur),
                  pl.BlockSpec((1, wq), lambda b, i: (0, 0))],
        out_specs=pl.BlockSpec((qblocks * BAND, wq), cur),
        out_shape=jax.ShapeDtypeStruct((t, wq), F32),
        compiler_params=_cparams(("parallel", "arbitrary")),
        name="band_attn",
    )(q, k, k, v, v, sink)


B_TILE = BAND * B_DILATIONS[-1]
B_WINDOW = ATTN_QBLOCKS * BAND


def _attend_items(items):
    lo = _iota((BAND, LANES), 1) < HEAD_DIM
    scores = []
    for q, k, _, _ in items:
        q2 = jnp.concatenate([jnp.where(lo, q, 0.0), jnp.where(lo, 0.0, q)], axis=0)
        scores.append(_dot_nt(q2, k))
    soft = []
    for (_, _, _, mask), sc in zip(items, scores):
        sc = jnp.where(mask, sc, NEG)
        m = jnp.max(sc, axis=-1, keepdims=True)
        p = jnp.exp(sc - m)
        den = jnp.sum(p, axis=-1, keepdims=True)
        soft.append((p, den, m + jnp.log(den)))
    outs = []
    for (_, _, v, _), (p, den, lse) in zip(items, soft):
        o2 = _dot(p, v) / den
        outs.append((jnp.where(lo, o2[:BAND], o2[BAND:]), jnp.where(lo, lse[:BAND], lse[BAND:])))
    return outs


def _merge_lse(o_old, l_old, o, l):
    m = jnp.maximum(l_old, l)
    w_old, w = jnp.exp(l_old - m), jnp.exp(l - m)
    den = w_old + w
    return (o_old * w_old + o * w) / den, m + jnp.log(den)


def _dilated_attn_kernel(q_ref, k_ref, v_ref, o_ref, kcat_ref, vcat_ref, qs_ref, os_ref, lse_ref,
                         qwin_ref, kwin_ref, vwin_ref, owin_ref, lwin_ref):
    i = pl.program_id(1)
    tiles = range(q_ref.shape[1] // LANES)
    lanes = lambda t: slice(t * LANES, (t + 1) * LANES)
    d1, d4, d16 = B_DILATIONS

    @pl.when(i == 0)
    def _():
        kcat_ref[:, :B_TILE] = jnp.zeros((len(tiles), B_TILE, LANES), F32)
        vcat_ref[:, :B_TILE] = jnp.zeros((len(tiles), B_TILE, LANES), F32)

    @pl.when(i > 0)
    def _():
        kcat_ref[:, :B_TILE] = kcat_ref[:, B_TILE:]
        vcat_ref[:, :B_TILE] = vcat_ref[:, B_TILE:]

    for t in tiles:
        kcat_ref[t, B_TILE:] = k_ref[:, lanes(t)]
        vcat_ref[t, B_TILE:] = v_ref[:, lanes(t)]
        qs_ref[t] = q_ref[:, lanes(t)] * (HEAD_DIM ** -0.5)

    qi = _iota((2 * BAND, 2 * BAND), 0) % BAND
    kj = _iota((2 * BAND, 2 * BAND), 1)
    band = (kj >= qi) & (kj <= qi + BAND)
    current = kj >= BAND

    def body1(w, carry):
        base = pl.multiple_of(w * B_WINDOW, B_WINDOW)
        back = pl.ds(pl.multiple_of(B_TILE - BAND + base, BAND), B_WINDOW + BAND)
        first = band & (current | (i > 0) | (w > 0))
        items = []
        for t in tiles:
            qw = qs_ref[t, pl.ds(base, B_WINDOW), :]
            kw = kcat_ref[t, back, :].astype(BF16)
            vw = vcat_ref[t, back, :].astype(BF16)
            items += [(qw[s * BAND:(s + 1) * BAND], kw[s * BAND:(s + 2) * BAND], vw[s * BAND:(s + 2) * BAND],
                       first if s == 0 else band) for s in range(ATTN_QBLOCKS)]
        for n, (o, lse) in enumerate(_attend_items(items)):
            t, s = divmod(n, ATTN_QBLOCKS)
            rows = pl.ds(pl.multiple_of(base + s * BAND, BAND), BAND)
            os_ref[t, rows, :] = o
            lse_ref[t, rows, :] = lse
        return carry

    lax.fori_loop(0, B_TILE // B_WINDOW, body1, 0)

    def body4(w, carry):
        span = BAND * d4
        base = pl.multiple_of(w * span, span)
        back = pl.ds(pl.multiple_of(B_TILE - span + base, span), 2 * span)
        qwin_ref[...] = qs_ref[:, pl.ds(base, span), :]
        kwin_ref[...] = kcat_ref[:, back, :]
        vwin_ref[...] = vcat_ref[:, back, :]
        owin_ref[...] = os_ref[:, pl.ds(base, span), :]
        lwin_ref[...] = lse_ref[:, pl.ds(base, span), :]
        mask = band & (current | (i > 0) | (w > 0))
        items = [(qwin_ref[t, pl.ds(r, BAND, stride=d4), :],
                  kwin_ref[t, pl.ds(r, 2 * BAND, stride=d4), :].astype(BF16),
                  vwin_ref[t, pl.ds(r, 2 * BAND, stride=d4), :].astype(BF16), mask) for t in tiles for r in range(d4)]
        for n, (o, lse) in enumerate(_attend_items(items)):
            t, r = divmod(n, d4)
            rows = pl.ds(r, BAND, stride=d4)
            o_new, l_new = _merge_lse(owin_ref[t, rows, :], lwin_ref[t, rows, :], o, lse)
            owin_ref[t, rows, :] = o_new
            lwin_ref[t, rows, :] = l_new
        os_ref[:, pl.ds(base, span), :] = owin_ref[...]
        lse_ref[:, pl.ds(base, span), :] = lwin_ref[...]
        return carry

    lax.fori_loop(0, B_TILE // (BAND * d4), body4, 0)

    mask = band & (current | (i > 0))
    for g in range(d16 // ATTN_QBLOCKS):
        classes = range(g * ATTN_QBLOCKS, (g + 1) * ATTN_QBLOCKS)
        items = [(qs_ref[t, pl.ds(r, BAND, stride=d16), :],
                  kcat_ref[t, pl.ds(r, 2 * BAND, stride=d16), :].astype(BF16),
                  vcat_ref[t, pl.ds(r, 2 * BAND, stride=d16), :].astype(BF16), mask) for t in tiles for r in classes]
        for n, (o, lse) in enumerate(_attend_items(items)):
            t, c = divmod(n, ATTN_QBLOCKS)
            rows = pl.ds(classes[c], BAND, stride=d16)
            os_ref[t, rows, :] = _merge_lse(os_ref[t, rows, :], lse_ref[t, rows, :], o, lse)[0]

    for t in tiles:
        o_ref[:, lanes(t)] = os_ref[t]


def dilated_attn(q, k, v, n):
    t, w = q.shape
    steps = t // n // B_TILE
    tile = pl.BlockSpec((B_TILE, w), lambda b, i: (b * steps + i, 0))
    tile_in = pl.BlockSpec((B_TILE, w), lambda b, i: (b * steps + i, 0), pipeline_mode=pl.Buffered(1))
    span4 = BAND * B_DILATIONS[1]
    nt = w // LANES
    return pl.pallas_call(
        _dilated_attn_kernel,
        grid=(n, steps),
        in_specs=[tile_in, tile_in, tile_in],
        out_specs=tile,
        out_shape=jax.ShapeDtypeStruct((t, w), F32),
        scratch_shapes=[pltpu.VMEM((nt, rows, LANES), F32) for rows in
                        (2 * B_TILE, 2 * B_TILE, B_TILE, B_TILE, B_TILE, span4, 2 * span4, 2 * span4, span4, span4)],
        compiler_params=_cparams(("parallel", "arbitrary")),
        name="dilated_attn",
    )(q, k, v)


NEW_PAD = 8


def _softmax_t(s_c, s_n, valid_c, valid_n, sink):
    s_c = jnp.where(valid_c, s_c, NEG)
    s_n = jnp.where(valid_n, s_n, NEG)
    m = jnp.maximum(jnp.max(s_c, axis=-1, keepdims=True), jnp.max(s_n, axis=-1, keepdims=True))
    if sink is not None:
        m = jnp.maximum(m, sink)
    p_c = jnp.exp(s_c - m)
    p_n = jnp.exp(s_n - m)
    den = jnp.sum(p_c, axis=-1, keepdims=True) + jnp.sum(p_n, axis=-1, keepdims=True)
    if sink is not None:
        den = den + jnp.exp(sink - m)
    return p_c, p_n, den, m + jnp.log(den)


def _sample_attn_kernel(qa_ref, kan_ref, van_ref, cak_ref, cav_ref, sink_ref,
                        qb_ref, kbn_ref, vbn_ref, cbk_ref, cbv_ref, oa_ref, ob_ref, *, n_new):
    pa = cak_ref.shape[-1]
    pb = cbk_ref.shape[-1]
    groups_a, heads_b = cak_ref.shape[2], cbk_ref.shape[2]
    scale = HEAD_DIM ** -0.5
    first = pl.program_id(0) % (LANES // n_new) * n_new
    new_i = _iota((NEW_PAD, LANES), 1) - first
    live_n = (new_i >= 0) & (new_i < n_new)
    jn = _iota((NEW_PAD, LANES), 0)
    head = lambda h: slice(h * HEAD_DIM, (h + 1) * HEAD_DIM)

    qa = [(qa_ref[0, g] * scale).astype(BF16) for g in range(groups_a)]
    qb = [(qb_ref[0, h] * scale).astype(BF16) for h in range(heads_b)]
    sa_c = [_dot(qa[g], cak_ref[0, 0, g]) for g in range(groups_a)]
    sa_n = [_dot(qa[g], kan_ref[head(g), :]) for g in range(groups_a)]
    sb_c = [_dot(qb[h], cbk_ref[0, 0, h]) for h in range(heads_b)]
    sb_n = [_dot(qb[h], kbn_ref[head(h), :]) for h in range(heads_b)]

    valid_c = _iota((NEW_PAD, pa), 1) >= _iota((NEW_PAD, pa), 0) % n_new
    valid_n = live_n & (new_i <= jn % n_new)
    soft_a = [_softmax_t(sa_c[g], sa_n[g], valid_c, valid_n, sink_ref[g][:, 0:1]) for g in range(groups_a)]
    soft_b = {}
    for d in B_DILATIONS:
        span = BAND * d
        jc = _iota((NEW_PAD, span), 0)
        pc = _iota((NEW_PAD, span), 1)
        if d == 1:
            valid_c, valid_n = pc >= jc, live_n & (new_i <= jn)
        else:
            valid_c, valid_n = pc % d == jc, live_n & (new_i == jn)
        for h in range(heads_b):
            soft_b[h, d] = _softmax_t(sb_c[h][:, pb - span:], sb_n[h], valid_c, valid_n, None)

    for g in range(groups_a):
        p_c, p_n, den, _ = soft_a[g]
        oa_ref[0, g] = (_dot_nt(p_c, cav_ref[0, 0, g]) + _dot_nt(p_n, van_ref[head(g), :])) / den
    outs = {}
    for h in range(heads_b):
        for d in B_DILATIONS:
            p_c, p_n, den, _ = soft_b[h, d]
            outs[h, d] = (_dot_nt(p_c, cbv_ref[0, 0, h, :, pb - BAND * d:]) + _dot_nt(p_n, vbn_ref[head(h), :])) / den
    for h in range(heads_b):
        lses = [soft_b[h, d][3] for d in B_DILATIONS]
        m = jnp.maximum(jnp.maximum(lses[0], lses[1]), lses[2])
        ws = [jnp.exp(l - m) for l in lses]
        o1, o2, o3 = [outs[h, d] for d in B_DILATIONS]
        ob_ref[0, h] = (ws[0] * o1 + ws[1] * o2 + ws[2] * o3) / (ws[0] + ws[1] + ws[2])


def _seg_ones(w, dtype):
    r = jnp.arange(w) // HEAD_DIM
    return (r[:, None] == r[None, :]).astype(dtype)


def _new_keys_t(x):
    return jnp.pad(x.T, ((0, 0), (0, -x.shape[0] % LANES)))


def sample_attn(qa, ka, va, cache_a_k_t, cache_a_v_t, sink, qb, kb, vb, cache_b_k_t, cache_b_v_t, layer, n, l):
    ga, gb = cache_a_k_t.shape[2], cache_b_k_t.shape[2]
    pa, pb = cache_a_k_t.shape[4], cache_b_k_t.shape[4]
    reps = A_Q // A_KV
    assert pa == BAND and pb == BAND * B_DILATIONS[-1] and l * reps == NEW_PAD and l <= B_DILATIONS[1]
    qa_g = qa.reshape(n, l, reps, ga, HEAD_DIM).transpose(0, 3, 2, 1, 4).reshape(n, ga, NEW_PAD, HEAD_DIM)
    sink_g = jnp.broadcast_to(sink.reshape(ga, reps, 1, 1), (ga, reps, l, LANES)).reshape(ga, NEW_PAD, LANES)
    qb_g = jnp.pad(qb.reshape(n, l, gb, HEAD_DIM).transpose(0, 2, 1, 3), ((0, 0), (0, 0), (0, NEW_PAD - l), (0, 0)))
    rows = lambda g: pl.BlockSpec((1, g, NEW_PAD, HEAD_DIM), lambda b: (b, 0, 0, 0))
    cols = lambda g: pl.BlockSpec((g * HEAD_DIM, LANES), lambda b: (0, b // (LANES // l)))
    cache = lambda g, p: pl.BlockSpec((1, 1, g, HEAD_DIM, p), lambda b: (layer, b, 0, 0, 0))
    oa, ob = pl.pallas_call(
        functools.partial(_sample_attn_kernel, n_new=l),
        grid=(n,),
        in_specs=[rows(ga), cols(ga), cols(ga), cache(ga, pa), cache(ga, pa),
                  pl.BlockSpec((ga, NEW_PAD, LANES), lambda b: (0, 0, 0)),
                  rows(gb), cols(gb), cols(gb), cache(gb, pb), cache(gb, pb)],
        out_specs=[rows(ga), rows(gb)],
        out_shape=[jax.ShapeDtypeStruct((n, ga, NEW_PAD, HEAD_DIM), F32),
                   jax.ShapeDtypeStruct((n, gb, NEW_PAD, HEAD_DIM), F32)],
        compiler_params=_cparams(("parallel",)),
        name="sample_attn",
    )(qa_g, _new_keys_t(ka), _new_keys_t(va), cache_a_k_t, cache_a_v_t, sink_g,
      qb_g, _new_keys_t(kb), _new_keys_t(vb), cache_b_k_t, cache_b_v_t)
    oa = oa.reshape(n, ga, reps, l, HEAD_DIM).transpose(0, 3, 2, 1, 4).reshape(n * l, A_Q)
    ob = ob[:, :, :l].transpose(0, 2, 1, 3).reshape(n * l, B_W)
    return oa, ob


def _rwkv_prep_kernel(z_ref, zfirst_ref, mu_ref, w0_ref, a0_ref, kk_ref, ka_ref, w2_ref, a2_ref, g2_ref, seg_ref,
                      r_out, k_out, v_out, al_out, b_out, lw_out, g_out, carry_ref, *, seq_len):
    tile = pl.program_id(1)
    z = z_ref[...]
    tm = z.shape[0]
    rolled = pltpu.roll(z, 1, axis=0)
    if seq_len >= tm:
        @pl.when(tile == 0)
        def _():
            carry_ref[...] = zfirst_ref[0]

        prev = jnp.where(_iota((tm, 1), 0) == 0, carry_ref[...], rolled)
        carry_ref[...] = z[tm - 1:tm, :]
    else:
        prev = jnp.where(_iota((tm, 1), 0) % seq_len == 0, zfirst_ref[...], rolled)
    zs = z + (prev - z) * mu_ref[...]
    r = zs[:, 0:C_W]
    k = zs[:, C_W:2 * C_W]
    v = zs[:, 2 * C_W:3 * C_W]
    lora = zs[:, 3 * C_W:3 * C_W + LANES]
    zg = zs[:, 3 * C_W + LANES:]
    w_log = -jax.nn.softplus(-(w0_ref[...] + _dot_x3(jnp.tanh(lora), w2_ref[...]))) - 0.5
    a = jax.nn.sigmoid(a0_ref[...] + _dot_x3(lora, a2_ref[...]))
    g = _dot_x3(jax.nn.sigmoid(zg), g2_ref[...])
    kk = k * kk_ref[...]
    kk = kk / jnp.maximum(jnp.sqrt(_dot_x2(kk * kk, seg_ref[...])), 1e-12)
    r_out[...] = r
    k_out[...] = k * (1.0 + (a - 1.0) * ka_ref[...])
    v_out[...] = v
    al_out[...] = -kk
    b_out[...] = kk * a
    lw_out[...] = -jnp.exp(w_log)
    g_out[...] = g


def rwkv_prep(zc, z_before, lw, n_seq):
    t = zc.shape[0]
    seq_len = t // n_seq
    tm = min(PREP_ROWS, t)
    if seq_len >= tm:
        grid = (n_seq, seq_len // tm)
        zfirst = z_before.reshape(n_seq, 1, C_PROJ)
        zfirst_spec = pl.BlockSpec((1, 1, C_PROJ), lambda b, i: (b, 0, 0))
    else:
        grid = (1, t // tm)
        zfirst = jnp.pad(z_before.reshape(n_seq, 1, C_PROJ), ((0, 0), (0, seq_len - 1), (0, 0))).reshape(t, C_PROJ)
        zfirst_spec = pl.BlockSpec((tm, C_PROJ), lambda b, i: (i, 0))
    tiles = grid[1]
    row = lambda b, i: (b * tiles + i, 0)
    const = lambda s: pl.BlockSpec(s, lambda b, i: (0, 0))
    vec = lambda a: a.reshape(1, -1)
    zero64 = jnp.zeros((HEAD_DIM, C_W), F32)
    w2p = jnp.concatenate([lw['c_w2'], zero64], axis=0)
    a2p = jnp.concatenate([zero64, lw['c_a2']], axis=0)
    out_spec = pl.BlockSpec((tm, C_W), row)
    return pl.pallas_call(
        functools.partial(_rwkv_prep_kernel, seq_len=seq_len),
        grid=grid,
        in_specs=[pl.BlockSpec((tm, C_PROJ), row),
                  zfirst_spec,
                  const((1, C_PROJ)), const((1, C_W)), const((1, C_W)), const((1, C_W)), const((1, C_W)),
                  const((LANES, C_W)), const((LANES, C_W)), const((LANES, C_W)), const((C_W, C_W))],
        out_specs=[out_spec] * 7,
        out_shape=[jax.ShapeDtypeStruct((t, C_W), F32)] * 7,
        scratch_shapes=[pltpu.VMEM((1, C_PROJ), F32)],
        compiler_params=_cparams(("parallel", "arbitrary")),
        name="rwkv_prep",
    )(zc, zfirst, vec(lw['c_mu']), vec(lw['c_w0']), vec(lw['c_a0']), vec(lw['c_k_k']), vec(lw['c_k_a']),
      w2p, a2p, lw['c_g2'], _seg_ones(C_W, F32))


SCAN_GROUP = 8
SCAN_PACKED_GROUP = 2


def _pair_block_diag(x, mask_b):
    xb = x.astype(BF16)
    return jnp.concatenate([xb, xb], axis=0) * mask_b


def _rwkv_scan_kernel(r_ref, k_ref, v_ref, al_ref, b_ref, lw_ref, g_ref, s0_ref, rk_ref, lnw_ref, lnb_ref, seg_ref,
                      o_ref, sfin_ref, s_ref, *, seq_len, group):
    step = pl.program_id(1)
    pairs = C_W // LANES
    rows = group * CHUNK
    chain = seq_len >= CHUNK
    span = min(seq_len, CHUNK)
    per = CHUNK // span
    ri = _iota((LANES, LANES), 0)
    ci = _iota((LANES, LANES), 1)
    bd_mask = (ri // HEAD_DIM) == (ci // HEAD_DIM)
    mask_b = jnp.where(bd_mask, 1.0, 0.0).astype(BF16)
    bd = lambda x: _pair_block_diag(x, mask_b)

    if chain:
        @pl.when(step == 0)
        def _():
            s_ref[...] = s0_ref[0]

    r, k, v, al, b, lw = r_ref[...], k_ref[...], v_ref[...], al_ref[...], b_ref[...], lw_ref[...]

    ti = _iota((rows, rows), 0)
    si = _iota((rows, rows), 1)
    same = ti // span == si // span
    tri = jnp.where((si <= ti) & same, 1.0, 0.0).astype(BF16)
    ones = jnp.where(same, 1.0, 0.0).astype(BF16)
    lw_hi, lw_lo = _split(lw)
    cum = jnp.dot(tri, lw_hi, preferred_element_type=F32) + jnp.dot(tri, lw_lo, preferred_element_type=F32)
    cum_end = jnp.dot(ones, lw_hi, preferred_element_type=F32) + jnp.dot(ones, lw_lo, preferred_element_type=F32)
    e_neg = jnp.exp(-cum)
    e_rem = jnp.exp(cum_end - cum)
    a_t = al * jnp.exp(cum - lw)
    b_t = b * e_neg
    k_t = k * e_neg
    r_t = r * jnp.exp(cum)
    b_h = b * e_rem
    k_h = k * e_rem

    tt = _iota((CHUNK, LANES), 0)
    ss = _iota((CHUNK, LANES), 1) % CHUNK
    together = tt // span == ss // span
    strict = (tt > ss) & together
    incl = (tt >= ss) & together
    eye = (tt == ss).astype(F32)
    items = [(c, m) for c in range(group) for m in range(pairs)]
    tile = lambda x, it: x[it[0] * CHUNK:(it[0] + 1) * CHUNK, it[1] * LANES:(it[1] + 1) * LANES]

    def independent(its, tick):
        gram = {}
        for it in its:
            ar = jnp.concatenate([tile(a_t, it), tile(r_t, it)], axis=0)
            bk = jnp.concatenate([bd(tile(b_t, it)), bd(tile(k_t, it))], axis=0)
            gram[it] = _dot_nt(ar, bk)
        tick()
        m_ab = {it: jnp.where(strict, gram[it][:CHUNK, :LANES], 0.0) for it in its}
        n_rb = {it: jnp.where(incl, gram[it][CHUNK:, :LANES], 0.0) for it in its}
        m_ak = {it: jnp.where(strict, gram[it][:CHUNK, LANES:], 0.0) for it in its}
        n_rk = {it: jnp.where(incl, gram[it][CHUNK:, LANES:], 0.0) for it in its}
        t_inv = {it: eye + jnp.where(tt // 2 == ss // 2, m_ab[it], 0.0) for it in its}
        blk = 4
        while blk <= span:
            joins = (tt // blk == ss // blk) & (tt // (blk // 2) != ss // (blk // 2))
            de = {it: _dot(t_inv[it], bd(jnp.where(joins, m_ab[it], 0.0))) for it in its}
            tick()
            t_inv = {it: t_inv[it] + _dot(de[it], bd(t_inv[it])) for it in its}
            tick()
            blk *= 2
        wv = {it: _dot(jnp.concatenate([m_ak[it], n_rk[it]], axis=0), bd(tile(v, it))) for it in its}
        tick()
        pq = {it: _dot(t_inv[it], jnp.concatenate([bd(tile(a_t, it)), bd(wv[it][:CHUNK])], axis=1)) for it in its}
        tick()
        return n_rb, wv, pq

    y = {}

    def dependent(chunks, parts, state):
        n_rb, wv, pq = parts
        heads = range(pairs)
        for c in chunks:
            uy = [_dot_nt(jnp.concatenate([pq[c, m][:, :LANES], tile(r_t, (c, m))], axis=0), state[m]) for m in heads]
            yield
            u = [uy[m][:CHUNK] + pq[c, m][:, LANES:] for m in heads]
            yd = [_dot(n_rb[c, m], bd(u[m])) for m in heads]
            upd = [_dot(jnp.concatenate([u[m], tile(v, (c, m))], axis=0).T,
                        jnp.concatenate([tile(b_h, (c, m)), tile(k_h, (c, m))], axis=0)) for m in heads]
            yield
            for m in heads:
                y[c, m] = uy[m][CHUNK:] + yd[m] + wv[c, m][CHUNK:]
                g_end = jnp.exp(tile(cum_end, (c, m))[0:1, :])
                state[m] = state[m] * g_end + jnp.where(bd_mask, upd[m], 0.0)
            yield

    if chain:
        state = [s_ref[m] for m in range(pairs)]
        halves = [range(group)] if group < 2 else [range(group // 2), range(group // 2, group)]
        pending = iter(())
        for half in halves:
            parts = independent([(c, m) for c in half for m in range(pairs)], lambda: next(pending, None))
            for _ in pending:
                pass
            pending = dependent(half, parts, state)
        for _ in pending:
            pass
        for m in range(pairs):
            s_ref[m] = state[m]
    else:
        n_rb, wv, pq = independent(items, lambda: None)
        row_seq = _iota((2 * CHUNK, 1), 0) % CHUNK // span
        col_seq = _iota((LANES, 2 * CHUNK), 1) % CHUNK // span
        for it in items:
            c, m = it
            pr = jnp.concatenate([pq[it][:, :LANES], tile(r_t, it)], axis=0)
            uy = None
            for j in range(per):
                part = _dot_nt(jnp.where(row_seq == j, pr, 0.0), s0_ref[c * per + j, m])
                uy = part if uy is None else uy + part
            u = uy[:CHUNK] + pq[it][:, LANES:]
            y[it] = uy[CHUNK:] + _dot(n_rb[it], bd(u)) + wv[it][CHUNK:]
            uv_t = jnp.concatenate([u, tile(v, it)], axis=0).T
            bk = jnp.concatenate([tile(b_h, it), tile(k_h, it)], axis=0).astype(BF16)
            for j in range(per):
                upd = _dot(jnp.where(col_seq == j, uv_t, 0.0), bk)
                g_end = jnp.exp(tile(cum_end, it)[j * span:j * span + 1, :])
                sfin_ref[c * per + j, m] = s0_ref[c * per + j, m] * g_end + jnp.where(bd_mask, upd, 0.0)

    seg = seg_ref[...]
    for m in range(pairs):
        sl = slice(m * LANES, (m + 1) * LANES)
        ym = jnp.concatenate([y[(c, m)] for c in range(group)], axis=0)
        mu = _dot_x2(ym, seg) * (1.0 / HEAD_DIM)
        dy = ym - mu
        var = _dot_x2(dy * dy, seg) * (1.0 / HEAD_DIM)
        yn = dy * lax.rsqrt(var + GN_EPS) * lnw_ref[:, sl] + lnb_ref[:, sl]
        bonus = _dot_x2(r[:, sl] * k[:, sl] * rk_ref[:, sl], seg)
        o_ref[:, sl] = (yn + bonus * v[:, sl]) * g_ref[:, sl]

    if chain:
        @pl.when(step == pl.num_programs(1) - 1)
        def _():
            sfin_ref[0] = s_ref[...]


def _state_to_pairs(s):
    n, h, hd, _ = s.shape
    s = s.reshape(n, h // 2, 2, hd, hd)
    bd = s[:, :, :, :, None, :] * jnp.eye(2, dtype=s.dtype)[None, None, :, None, :, None]
    return bd.reshape(n, h // 2, 2 * hd, 2 * hd)


def _pairs_to_state(bd):
    n, p, w, _ = bd.shape
    hd = w // 2
    bd = bd.reshape(n, p, 2, hd, 2, hd)
    return jnp.stack([bd[:, :, 0, :, 0, :], bd[:, :, 1, :, 1, :]], axis=2).reshape(n, 2 * p, hd, hd)


def rwkv_scan(prep, s0, lw, n_seq):
    t = prep[0].shape[0]
    seq_len = t // n_seq
    pairs = C_W // LANES
    if seq_len >= CHUNK:
        group = min(SCAN_GROUP, seq_len // CHUNK)
        grid = (n_seq, seq_len // CHUNK // group)
        n_state = 1
    else:
        group = min(SCAN_PACKED_GROUP, t // CHUNK)
        grid = (t // CHUNK // group, 1)
        n_state = group * (CHUNK // seq_len)
    steps = grid[1]
    row = pl.BlockSpec((group * CHUNK, C_W), lambda b, c: (b * steps + c, 0))
    const = lambda s: pl.BlockSpec(s, lambda b, c: (0, 0))
    state = pl.BlockSpec((n_state, pairs, LANES, LANES), lambda b, c: (b, 0, 0, 0))
    vec = lambda a: a.reshape(1, -1)
    out, s_fin = pl.pallas_call(
        functools.partial(_rwkv_scan_kernel, seq_len=seq_len, group=group),
        grid=grid,
        in_specs=[row] * 7 + [state, const((1, C_W)), const((1, C_W)), const((1, C_W)), const((LANES, LANES))],
        out_specs=[row, state],
        out_shape=[jax.ShapeDtypeStruct((t, C_W), F32), jax.ShapeDtypeStruct((n_seq, pairs, LANES, LANES), F32)],
        scratch_shapes=[pltpu.VMEM((pairs, LANES, LANES), F32)],
        compiler_params=_cparams(("parallel", "arbitrary")),
        name="rwkv_scan",
    )(*prep, _state_to_pairs(s0), vec(lw['c_r_k']), vec(lw['c_ln_w']), vec(lw['c_ln_b']), _seg_ones(LANES, F32))
    return out, _pairs_to_state(s_fin)


def _out_proj_kernel(x_ref, oa_ref, ob_ref, oc_ref, w_ref, y_ref):
    acc = x_ref[...]
    acc += jnp.dot(oa_ref[...].astype(BF16), w_ref[0:A_Q, :], preferred_element_type=F32)
    acc += jnp.dot(ob_ref[...].astype(BF16), w_ref[A_Q:A_Q + B_W, :], preferred_element_type=F32)
    acc += jnp.dot(oc_ref[...].astype(BF16), w_ref[A_Q + B_W:, :], preferred_element_type=F32)
    y_ref[...] = acc


def out_proj(x, oa, ob, oc, w_bf16, tm):
    t, d = x.shape
    row = lambda w: pl.BlockSpec((tm, w), lambda i: (i, 0))
    return pl.pallas_call(
        _out_proj_kernel, grid=(t // tm,),
        in_specs=[row(d), row(A_Q), row(B_W), row(C_W), pl.BlockSpec(w_bf16.shape, lambda i: (0, 0))],
        out_specs=row(d), out_shape=jax.ShapeDtypeStruct((t, d), F32),
        compiler_params=_cparams(("parallel",)), name="out_proj",
    )(x, oa, ob, oc, w_bf16)


FF_CHUNK = 1024


def _mlp_kernel(x_ref, g_ref, up_ref, down_ref, y_ref):
    x = x_ref[...]
    h = (x * lax.rsqrt(jnp.mean(x * x, axis=-1, keepdims=True) + RMS_EPS) * g_ref[...]).astype(BF16)
    acc = x
    for c in range(up_ref.shape[1] // FF_CHUNK):
        u = jnp.dot(h, up_ref[:, c * FF_CHUNK:(c + 1) * FF_CHUNK], preferred_element_type=F32)
        u = jnp.square(jnp.maximum(u, 0.0)).astype(BF16)
        acc += jnp.dot(u, down_ref[c * FF_CHUNK:(c + 1) * FF_CHUNK, :], preferred_element_type=F32)
    y_ref[...] = acc


def mlp(x, g, up_bf16, down_bf16, tm):
    t, d = x.shape
    row = pl.BlockSpec((tm, d), lambda i: (i, 0))
    return pl.pallas_call(
        _mlp_kernel, grid=(t // tm,),
        in_specs=[row, pl.BlockSpec((1, d), lambda i: (0, 0)),
                  pl.BlockSpec(up_bf16.shape, lambda i: (0, 0), pipeline_mode=pl.Buffered(1)),
                  pl.BlockSpec(down_bf16.shape, lambda i: (0, 0), pipeline_mode=pl.Buffered(1))],
        out_specs=row, out_shape=jax.ShapeDtypeStruct((t, d), F32),
        compiler_params=_cparams(("parallel",)), name="mlp",
    )(x, g.reshape(1, d), up_bf16, down_bf16)


def _final_norm_kernel(x_ref, g_ref, y_ref):
    x = x_ref[...]
    y_ref[...] = x * lax.rsqrt(jnp.mean(x * x, axis=-1, keepdims=True) + RMS_EPS) * g_ref[...]


def final_norm(x, g, tm):
    t, d = x.shape
    row = pl.BlockSpec((tm, d), lambda i: (i, 0))
    return pl.pallas_call(
        _final_norm_kernel, grid=(t // tm,),
        in_specs=[row, pl.BlockSpec((1, d), lambda i: (0, 0))],
        out_specs=row, out_shape=jax.ShapeDtypeStruct((t, d), F32),
        compiler_params=_cparams(("parallel",)), name="final_norm",
    )(x, g.reshape(1, d))


PREP_ROWS = 256
A_PERM = (0, 2, 1, 3)


def _permute_a_heads(w, axis):
    take = lambda a, b: lax.slice_in_dim(w, a, b, axis=axis)
    parts = [take(h * HEAD_DIM, (h + 1) * HEAD_DIM) for h in A_PERM] + [take(A_Q, w.shape[axis])]
    return jnp.concatenate(parts, axis=axis)


def _row_tile(t):
    return 512 if t % 512 == 0 else t


def trunk_layer(x, lw, n, past, layer):
    t = x.shape[0]
    l = t // n
    tm = _row_tile(t)
    if past is None:
        qa, ka, va, qb, kb, vb, zc, *tails = norm_proj(x, lw['norm1_g'], lw['w_in'], n, True)
        oa = band_attn(qa, ka, va, lw['sink'], n=n, kv_of_q=(0, 0))
        ob = dilated_attn(qb, kb, vb, n)
        z_before = jnp.zeros((n, C_PROJ), F32)
        s0 = jnp.zeros((n, C_W // HEAD_DIM, HEAD_DIM, HEAD_DIM), F32)
        windows = [a.reshape(n, -1, HEAD_DIM, a.shape[2]).transpose(0, 3, 1, 2) for a in tails]
    else:
        qa, ka, va, qb, kb, vb, zc = norm_proj(x, lw['norm1_g'], lw['w_in'], 1, False)
        cache_a_k_t, cache_a_v_t, cache_b_k_t, cache_b_v_t, state_wkv, state_shift = past
        oa, ob = sample_attn(qa, ka, va, cache_a_k_t, cache_a_v_t, lw['a_sink'],
                             qb, kb, vb, cache_b_k_t, cache_b_v_t, layer, n, l)
        z_before, s0 = state_shift[layer], state_wkv[layer]
        windows = [a.reshape(n, l, -1, HEAD_DIM) for a in (ka, va, kb, vb)]
    oc, s_fin = rwkv_scan(rwkv_prep(zc, z_before, lw, n), s0, lw, n)
    x = out_proj(x, oa, ob, oc, lw['w_out'], tm)
    x = mlp(x, lw['norm2_g'], lw['mlp_up'], lw['mlp_down'], tm)
    return x, (*windows, s_fin, zc.reshape(n, l, C_PROJ)[:, -1])


def kernel(x_prompt, x_sample, cache_a_k, cache_a_v, cache_b_k, cache_b_v, state_c_wkv, state_c_shift, norm1_g, norm2_g, w_in, w_out, a_sink, c_mu, c_w0, c_w2, c_a0, c_a2, c_g2, c_k_k, c_k_a, c_r_k, c_ln_w, c_ln_b, mlp_up, mlp_down, final_norm_g):
    depth = w_in.shape[0]
    n_p, l_p, d_model = x_prompt.shape
    n_s, l_s, _ = x_sample.shape
    w_in_b = _permute_a_heads(w_in, 2).astype(BF16)
    w_out_b = _permute_a_heads(w_out, 1).astype(BF16)
    up_b = mlp_up.astype(BF16)
    down_b = mlp_down.astype(BF16)
    sink = jnp.repeat(a_sink[:, jnp.array(A_PERM)], HEAD_DIM, axis=1).reshape(depth, 1, A_Q)
    cache_t = lambda c: jnp.transpose(c, (0, 1, 3, 4, 2))
    past = (cache_t(cache_a_k), cache_t(cache_a_v), cache_t(cache_b_k), cache_t(cache_b_v), state_c_wkv, state_c_shift)

    yp = x_prompt.reshape(n_p * l_p, d_model)
    ys = x_sample.reshape(n_s * l_s, d_model)
    p_new, s_new = [], []
    for l in range(depth):
        lw = {'norm1_g': norm1_g[l], 'norm2_g': norm2_g[l], 'w_in': w_in_b[l], 'w_out': w_out_b[l], 'sink': sink[l], 'a_sink': a_sink[l],
              'c_mu': c_mu[l], 'c_w0': c_w0[l], 'c_w2': c_w2[l], 'c_a0': c_a0[l], 'c_a2': c_a2[l], 'c_g2': c_g2[l],
              'c_k_k': c_k_k[l], 'c_k_a': c_k_a[l], 'c_r_k': c_r_k[l], 'c_ln_w': c_ln_w[l], 'c_ln_b': c_ln_b[l],
              'mlp_up': up_b[l], 'mlp_down': down_b[l]}
        yp, st_p = trunk_layer(yp, lw, n_p, None, l)
        ys, st_s = trunk_layer(ys, lw, n_s, past, l)
        p_new.append(st_p)
        s_new.append(st_s)
    p_state = [jnp.stack(t) for t in zip(*p_new)]
    s_state = [jnp.stack(t) for t in zip(*s_new)]
    y_prompt = final_norm(yp, final_norm_g, _row_tile(yp.shape[0])).reshape(n_p, l_p, d_model)
    y_sample = final_norm(ys, final_norm_g, _row_tile(ys.shape[0])).reshape(n_s, l_s, d_model)
    return (y_prompt, y_sample, *p_state, *s_state)
```

```python
import functools

import jax
import jax.numpy as jnp
from jax import lax
from jax.experimental import pallas as pl
from jax.experimental.pallas import tpu as pltpu

F32 = jnp.float32
BF16 = jnp.bfloat16

LANES = 128
HEAD_DIM = 64
BAND = 128
CHUNK = 64
RMS_EPS = 1e-5
GN_EPS = 64e-5
NEG = -1e30
VMEM_LIMIT = 56 * 1024 * 1024

A_Q, A_KV, B_W, C_W = 256, 128, 384, 384
C_PROJ = 1408
SPLITS = (0, 256, 384, 512, 896, 1280, 1664, 3072)
B_DILATIONS = (1, 4, 16)


def _cparams(sem):
    return pltpu.CompilerParams(dimension_semantics=sem, vmem_limit_bytes=VMEM_LIMIT)


def _dot(a, b):
    return jnp.dot(a.astype(BF16), b.astype(BF16), preferred_element_type=F32)


def _dot_nt(a, b):
    return lax.dot_general(a.astype(BF16), b.astype(BF16), (((1,), (1,)), ((), ())),
                           preferred_element_type=F32)


def _split(a):
    hi = a.astype(BF16)
    return hi, (a - hi.astype(F32)).astype(BF16)


def _dot_x2(a, b_exact):
    hi, lo = _split(a)
    b = b_exact.astype(BF16)
    return jnp.dot(hi, b, preferred_element_type=F32) + jnp.dot(lo, b, preferred_element_type=F32)


def _dot_x3(a, b):
    ah, al = _split(a)
    bh, bl = _split(b)
    return (jnp.dot(ah, bh, preferred_element_type=F32) + jnp.dot(al, bh, preferred_element_type=F32)
            + jnp.dot(ah, bl, preferred_element_type=F32))


def _iota(shape, dim):
    return lax.broadcasted_iota(jnp.int32, shape, dim)


KV_SPLITS = (1, 2, 4, 5)


def _norm_proj_kernel(x_ref, g_ref, w_ref, *out_refs, tail_a):
    x = x_ref[...]
    h = x * lax.rsqrt(jnp.mean(x * x, axis=-1, keepdims=True) + RMS_EPS) * g_ref[...]
    hb = h.astype(BF16)
    vals = []
    for o_ref, a, b in zip(out_refs, SPLITS[:-1], SPLITS[1:]):
        vals.append(jnp.dot(hb, w_ref[:, a:b], preferred_element_type=F32))
        o_ref[...] = vals[-1]
    if tail_a:
        tm = x.shape[0]
        ka_t, va_t, kb_t, vb_t = out_refs[len(vals):]
        ka_t[0] = vals[KV_SPLITS[0]][tm - tail_a:, :].T
        va_t[0] = vals[KV_SPLITS[1]][tm - tail_a:, :].T
        kb_t[0] = vals[KV_SPLITS[2]].T
        vb_t[0] = vals[KV_SPLITS[3]].T


def norm_proj(x, g, w_in_b, n, tails):
    t, d = x.shape
    l = t // n
    tm = _row_tile(l)
    tiles = l // tm
    widths = [b - a for a, b in zip(SPLITS[:-1], SPLITS[1:])]
    row = lambda b, i: (b * tiles + i, 0)
    out_specs = [pl.BlockSpec((tm, w), row) for w in widths]
    out_shape = [jax.ShapeDtypeStruct((t, w), F32) for w in widths]
    tail_a = min(BAND, l) if tails else 0
    if tails:
        tail_b = min(BAND * B_DILATIONS[-1], l)
        skip = (l - tail_b) // tm
        out_specs += [pl.BlockSpec((1, A_KV, tail_a), lambda b, i: (b, 0, 0))] * 2
        out_specs += [pl.BlockSpec((1, B_W, tm), lambda b, i: (b, 0, jnp.maximum(i - skip, 0)))] * 2
        out_shape += [jax.ShapeDtypeStruct((n, A_KV, tail_a), F32)] * 2
        out_shape += [jax.ShapeDtypeStruct((n, B_W, tail_b), F32)] * 2
    return pl.pallas_call(
        functools.partial(_norm_proj_kernel, tail_a=tail_a),
        grid=(n, tiles),
        in_specs=[pl.BlockSpec((tm, d), row),
                  pl.BlockSpec((1, d), lambda b, i: (0, 0)),
                  pl.BlockSpec(w_in_b.shape, lambda b, i: (0, 0))],
        out_specs=out_specs,
        out_shape=out_shape,
        compiler_params=_cparams(("parallel", "arbitrary")),
        name="norm_proj",
    )(x, g.reshape(1, d), w_in_b)


ATTN_QBLOCKS = 4


def _band_attn_kernel(q_ref, kp_ref, kc_ref, vp_ref, vc_ref, sink_ref, o_ref, *, nq, kv_of_q, qblocks):
    i = pl.program_id(1)
    qi = _iota((2 * BAND, 2 * BAND), 0) % BAND
    kj = _iota((2 * BAND, 2 * BAND), 1)
    band = (kj >= qi) & (kj <= qi + BAND)
    band_first = band & ((kj >= BAND) | (i > 0))
    first_head = _iota((2 * BAND, 1), 0) < BAND
    lo = _iota((BAND, LANES), 1) < HEAD_DIM
    nkv = kp_ref.shape[1] // LANES
    lanes = lambda t: slice(t * LANES, (t + 1) * LANES)
    kcat = [jnp.concatenate([kp_ref[:, lanes(t)], kc_ref[:, lanes(t)]], axis=0).astype(BF16) for t in range(nkv)]
    vcat = [jnp.concatenate([vp_ref[:, lanes(t)], vc_ref[:, lanes(t)]], axis=0).astype(BF16) for t in range(nkv)]
    items = [(s, t) for s in range(qblocks) for t in range(nq)]
    rows = lambda s: slice(s * BAND, (s + 1) * BAND)
    keys = lambda s: slice(s * BAND, (s + 2) * BAND)

    scores = {}
    for s, t in items:
        q = q_ref[rows(s), lanes(t)] * (HEAD_DIM ** -0.5)
        q2 = jnp.concatenate([jnp.where(lo, q, 0.0), jnp.where(lo, 0.0, q)], axis=0)
        scores[s, t] = _dot_nt(q2, kcat[kv_of_q[t]][keys(s)])
    probs, dens = {}, {}
    for s, t in items:
        sc = jnp.where(band_first if s == 0 else band, scores[s, t], NEG)
        c = t * LANES
        sk = jnp.where(first_head, sink_ref[0:1, c:c + 1], sink_ref[0:1, c + HEAD_DIM:c + HEAD_DIM + 1])
        m = jnp.maximum(jnp.max(sc, axis=-1, keepdims=True), sk)
        p = jnp.exp(sc - m)
        probs[s, t], dens[s, t] = p, jnp.sum(p, axis=-1, keepdims=True) + jnp.exp(sk - m)
    for s, t in items:
        o2 = _dot(probs[s, t], vcat[kv_of_q[t]][keys(s)]) / dens[s, t]
        o_ref[rows(s), lanes(t)] = jnp.where(lo, o2[:BAND], o2[BAND:])


def band_attn(q, k, v, sink, *, n, kv_of_q):
    t, wq = q.shape
    wkv = k.shape[1]
    nb = t // n // BAND
    qblocks = min(ATTN_QBLOCKS, nb)
    steps = nb // qblocks
    cur = lambda b, i: (b * steps + i, 0)
    prev = lambda b, i: (b * nb + jnp.maximum(i * qblocks - 1, 0), 0)
    return pl.pallas_call(
        functools.partial(_band_attn_kernel, nq=wq // LANES, kv_of_q=kv_of_q, qblocks=qblocks),
        grid=(n, steps),
        in_specs=[pl.BlockSpec((qblocks * BAND, wq), cur),
                  pl.BlockSpec((BAND, wkv), prev), pl.BlockSpec((qblocks * BAND, wkv), cur),
                  pl.BlockSpec((BAND, wkv), prev), pl.BlockSpec((qblocks * BAND, wkv), cur),
                  pl.BlockSpec((1, wq), lambda b, i: (0, 0))],
        out_specs=pl.BlockSpec((qblocks * BAND, wq), cur),
        out_shape=jax.ShapeDtypeStruct((t, wq), F32),
        compiler_params=_cparams(("parallel", "arbitrary")),
        name="band_attn",
    )(q, k, k, v, v, sink)


B_TILE = BAND * B_DILATIONS[-1]
B_WINDOW = ATTN_QBLOCKS * BAND


def _attend_items(items):
    lo = _iota((BAND, LANES), 1) < HEAD_DIM
    scores = []
    for q, k, _, _ in items:
        q2 = jnp.concatenate([jnp.where(lo, q, 0.0), jnp.where(lo, 0.0, q)], axis=0)
        scores.append(_dot_nt(q2, k))
    soft = []
    for (_, _, _, mask), sc in zip(items, scores):
        sc = jnp.where(mask, sc, NEG)
        m = jnp.max(sc, axis=-1, keepdims=True)
        p = jnp.exp(sc - m)
        den = jnp.sum(p, axis=-1, keepdims=True)
        soft.append((p, den, m + jnp.log(den)))
    outs = []
    for (_, _, v, _), (p, den, lse) in zip(items, soft):
        o2 = _dot(p, v) / den
        outs.append((jnp.where(lo, o2[:BAND], o2[BAND:]), jnp.where(lo, lse[:BAND], lse[BAND:])))
    return outs


def _merge_lse(o_old, l_old, o, l):
    m = jnp.maximum(l_old, l)
    w_old, w = jnp.exp(l_old - m), jnp.exp(l - m)
    den = w_old + w
    return (o_old * w_old + o * w) / den, m + jnp.log(den)


def _dilated_attn_kernel(q_ref, k_ref, v_ref, o_ref, kcat_ref, vcat_ref, qs_ref, os_ref, lse_ref,
                         qwin_ref, kwin_ref, vwin_ref, owin_ref, lwin_ref):
    i = pl.program_id(1)
    tiles = range(q_ref.shape[1] // LANES)
    lanes = lambda t: slice(t * LANES, (t + 1) * LANES)
    d1, d4, d16 = B_DILATIONS

    @pl.when(i == 0)
    def _():
        kcat_ref[:, :B_TILE] = jnp.zeros((len(tiles), B_TILE, LANES), F32)
        vcat_ref[:, :B_TILE] = jnp.zeros((len(tiles), B_TILE, LANES), F32)

    @pl.when(i > 0)
    def _():
        kcat_ref[:, :B_TILE] = kcat_ref[:, B_TILE:]
        vcat_ref[:, :B_TILE] = vcat_ref[:, B_TILE:]

    for t in tiles:
        kcat_ref[t, B_TILE:] = k_ref[:, lanes(t)]
        vcat_ref[t, B_TILE:] = v_ref[:, lanes(t)]
        qs_ref[t] = q_ref[:, lanes(t)] * (HEAD_DIM ** -0.5)

    qi = _iota((2 * BAND, 2 * BAND), 0) % BAND
    kj = _iota((2 * BAND, 2 * BAND), 1)
    band = (kj >= qi) & (kj <= qi + BAND)
    current = kj >= BAND

    def body1(w, carry):
        base = pl.multiple_of(w * B_WINDOW, B_WINDOW)
        back = pl.ds(pl.multiple_of(B_TILE - BAND + base, BAND), B_WINDOW + BAND)
        first = band & (current | (i > 0) | (w > 0))
        items = []
        for t in tiles:
            qw = qs_ref[t, pl.ds(base, B_WINDOW), :]
            kw = kcat_ref[t, back, :].astype(BF16)
            vw = vcat_ref[t, back, :].astype(BF16)
            items += [(qw[s * BAND:(s + 1) * BAND], kw[s * BAND:(s + 2) * BAND], vw[s * BAND:(s + 2) * BAND],
                       first if s == 0 else band) for s in range(ATTN_QBLOCKS)]
        for n, (o, lse) in enumerate(_attend_items(items)):
            t, s = divmod(n, ATTN_QBLOCKS)
            rows = pl.ds(pl.multiple_of(base + s * BAND, BAND), BAND)
            os_ref[t, rows, :] = o
            lse_ref[t, rows, :] = lse
        return carry

    lax.fori_loop(0, B_TILE // B_WINDOW, body1, 0)

    def body4(w, carry):
        span = BAND * d4
        base = pl.multiple_of(w * span, span)
        back = pl.ds(pl.multiple_of(B_TILE - span + base, span), 2 * span)
        qwin_ref[...] = qs_ref[:, pl.ds(base, span), :]
        kwin_ref[...] = kcat_ref[:, back, :]
        vwin_ref[...] = vcat_ref[:, back, :]
        owin_ref[...] = os_ref[:, pl.ds(base, span), :]
        lwin_ref[...] = lse_ref[:, pl.ds(base, span), :]
        mask = band & (current | (i > 0) | (w > 0))
        items = [(qwin_ref[t, pl.ds(r, BAND, stride=d4), :],
                  kwin_ref[t, pl.ds(r, 2 * BAND, stride=d4), :].astype(BF16),
                  vwin_ref[t, pl.ds(r, 2 * BAND, stride=d4), :].astype(BF16), mask) for t in tiles for r in range(d4)]
        for n, (o, lse) in enumerate(_attend_items(items)):
            t, r = divmod(n, d4)
            rows = pl.ds(r, BAND, stride=d4)
            o_new, l_new = _merge_lse(owin_ref[t, rows, :], lwin_ref[t, rows, :], o, lse)
            owin_ref[t, rows, :] = o_new
            lwin_ref[t, rows, :] = l_new
        os_ref[:, pl.ds(base, span), :] = owin_ref[...]
        lse_ref[:, pl.ds(base, span), :] = lwin_ref[...]
        return carry

    lax.fori_loop(0, B_TILE // (BAND * d4), body4, 0)

    mask = band & (current | (i > 0))
    for g in range(d16 // ATTN_QBLOCKS):
        classes = range(g * ATTN_QBLOCKS, (g + 1) * ATTN_QBLOCKS)
        items = [(qs_ref[t, pl.ds(r, BAND, stride=d16), :],
                  kcat_ref[t, pl.ds(r, 2 * BAND, stride=d16), :].astype(BF16),
                  vcat_ref[t, pl.ds(r, 2 * BAND, stride=d16), :].astype(BF16), mask) for t in tiles for r in classes]
        for n, (o, lse) in enumerate(_attend_items(items)):
            t, c = divmod(n, ATTN_QBLOCKS)
            rows = pl.ds(classes[c], BAND, stride=d16)
            os_ref[t, rows, :] = _merge_lse(os_ref[t, rows, :], lse_ref[t, rows, :], o, lse)[0]

    for t in tiles:
        o_ref[:, lanes(t)] = os_ref[t]


def dilated_attn(q, k, v, n):
    t, w = q.shape
    steps = t // n // B_TILE
    tile = pl.BlockSpec((B_TILE, w), lambda b, i: (b * steps + i, 0))
    tile_in = pl.BlockSpec((B_TILE, w), lambda b, i: (b * steps + i, 0), pipeline_mode=pl.Buffered(1))
    span4 = BAND * B_DILATIONS[1]
    nt = w // LANES
    return pl.pallas_call(
        _dilated_attn_kernel,
        grid=(n, steps),
        in_specs=[tile_in, tile_in, tile_in],
        out_specs=tile,
        out_shape=jax.ShapeDtypeStruct((t, w), F32),
        scratch_shapes=[pltpu.VMEM((nt, rows, LANES), F32) for rows in
                        (2 * B_TILE, 2 * B_TILE, B_TILE, B_TILE, B_TILE, span4, 2 * span4, 2 * span4, span4, span4)],
        compiler_params=_cparams(("parallel", "arbitrary")),
        name="dilated_attn",
    )(q, k, v)


NEW_PAD = 8


def _softmax_t(s_c, s_n, valid_c, valid_n, sink):
    s_c = jnp.where(valid_c, s_c, NEG)
    s_n = jnp.where(valid_n, s_n, NEG)
    m = jnp.maximum(jnp.max(s_c, axis=-1, keepdims=True), jnp.max(s_n, axis=-1, keepdims=True))
    if sink is not None:
        m = jnp.maximum(m, sink)
    p_c = jnp.exp(s_c - m)
    p_n = jnp.exp(s_n - m)
    den = jnp.sum(p_c, axis=-1, keepdims=True) + jnp.sum(p_n, axis=-1, keepdims=True)
    if sink is not None:
        den = den + jnp.exp(sink - m)
    return p_c, p_n, den, m + jnp.log(den)


def _sample_attn_kernel(qa_ref, kan_ref, van_ref, cak_ref, cav_ref, sink_ref,
                        qb_ref, kbn_ref, vbn_ref, cbk_ref, cbv_ref, oa_ref, ob_ref, *, n_new):
    pa = cak_ref.shape[-1]
    pb = cbk_ref.shape[-1]
    groups_a, heads_b = cak_ref.shape[2], cbk_ref.shape[2]
    scale = HEAD_DIM ** -0.5
    first = pl.program_id(0) % (LANES // n_new) * n_new
    new_i = _iota((NEW_PAD, LANES), 1) - first
    live_n = (new_i >= 0) & (new_i < n_new)
    jn = _iota((NEW_PAD, LANES), 0)
    head = lambda h: slice(h * HEAD_DIM, (h + 1) * HEAD_DIM)

    qa = [(qa_ref[0, g] * scale).astype(BF16) for g in range(groups_a)]
    qb = [(qb_ref[0, h] * scale).astype(BF16) for h in range(heads_b)]
    sa_c = [_dot(qa[g], cak_ref[0, 0, g]) for g in range(groups_a)]
    sa_n = [_dot(qa[g], kan_ref[head(g), :]) for g in range(groups_a)]
    sb_c = [_dot(qb[h], cbk_ref[0, 0, h]) for h in range(heads_b)]
    sb_n = [_dot(qb[h], kbn_ref[head(h), :]) for h in range(heads_b)]

    valid_c = _iota((NEW_PAD, pa), 1) >= _iota((NEW_PAD, pa), 0) % n_new
    valid_n = live_n & (new_i <= jn % n_new)
    soft_a = [_softmax_t(sa_c[g], sa_n[g], valid_c, valid_n, sink_ref[g][:, 0:1]) for g in range(groups_a)]
    soft_b = {}
    for d in B_DILATIONS:
        span = BAND * d
        jc = _iota((NEW_PAD, span), 0)
        pc = _iota((NEW_PAD, span), 1)
        if d == 1:
            valid_c, valid_n = pc >= jc, live_n & (new_i <= jn)
        else:
            valid_c, valid_n = pc % d == jc, live_n & (new_i == jn)
        for h in range(heads_b):
            soft_b[h, d] = _softmax_t(sb_c[h][:, pb - span:], sb_n[h], valid_c, valid_n, None)

    for g in range(groups_a):
        p_c, p_n, den, _ = soft_a[g]
        oa_ref[0, g] = (_dot_nt(p_c, cav_ref[0, 0, g]) + _dot_nt(p_n, van_ref[head(g), :])) / den
    outs = {}
    for h in range(heads_b):
        for d in B_DILATIONS:
            p_c, p_n, den, _ = soft_b[h, d]
            outs[h, d] = (_dot_nt(p_c, cbv_ref[0, 0, h, :, pb - BAND * d:]) + _dot_nt(p_n, vbn_ref[head(h), :])) / den
    for h in range(heads_b):
        lses = [soft_b[h, d][3] for d in B_DILATIONS]
        m = jnp.maximum(jnp.maximum(lses[0], lses[1]), lses[2])
        ws = [jnp.exp(l - m) for l in lses]
        o1, o2, o3 = [outs[h, d] for d in B_DILATIONS]
        ob_ref[0, h] = (ws[0] * o1 + ws[1] * o2 + ws[2] * o3) / (ws[0] + ws[1] + ws[2])


def _seg_ones(w, dtype):
    r = jnp.arange(w) // HEAD_DIM
    return (r[:, None] == r[None, :]).astype(dtype)


def _new_keys_t(x):
    return jnp.pad(x.T, ((0, 0), (0, -x.shape[0] % LANES)))


def sample_attn(qa, ka, va, cache_a_k_t, cache_a_v_t, sink, qb, kb, vb, cache_b_k_t, cache_b_v_t, layer, n, l):
    ga, gb = cache_a_k_t.shape[2], cache_b_k_t.shape[2]
    pa, pb = cache_a_k_t.shape[4], cache_b_k_t.shape[4]
    reps = A_Q // A_KV
    assert pa == BAND and pb == BAND * B_DILATIONS[-1] and l * reps == NEW_PAD and l <= B_DILATIONS[1]
    qa_g = qa.reshape(n, l, reps, ga, HEAD_DIM).transpose(0, 3, 2, 1, 4).reshape(n, ga, NEW_PAD, HEAD_DIM)
    sink_g = jnp.broadcast_to(sink.reshape(ga, reps, 1, 1), (ga, reps, l, LANES)).reshape(ga, NEW_PAD, LANES)
    qb_g = jnp.pad(qb.reshape(n, l, gb, HEAD_DIM).transpose(0, 2, 1, 3), ((0, 0), (0, 0), (0, NEW_PAD - l), (0, 0)))
    rows = lambda g: pl.BlockSpec((1, g, NEW_PAD, HEAD_DIM), lambda b: (b, 0, 0, 0))
    cols = lambda g: pl.BlockSpec((g * HEAD_DIM, LANES), lambda b: (0, b // (LANES // l)))
    cache = lambda g, p: pl.BlockSpec((1, 1, g, HEAD_DIM, p), lambda b: (layer, b, 0, 0, 0))
    oa, ob = pl.pallas_call(
        functools.partial(_sample_attn_kernel, n_new=l),
        grid=(n,),
        in_specs=[rows(ga), cols(ga), cols(ga), cache(ga, pa), cache(ga, pa),
                  pl.BlockSpec((ga, NEW_PAD, LANES), lambda b: (0, 0, 0)),
                  rows(gb), cols(gb), cols(gb), cache(gb, pb), cache(gb, pb)],
        out_specs=[rows(ga), rows(gb)],
        out_shape=[jax.ShapeDtypeStruct((n, ga, NEW_PAD, HEAD_DIM), F32),
                   jax.ShapeDtypeStruct((n, gb, NEW_PAD, HEAD_DIM), F32)],
        compiler_params=_cparams(("parallel",)),
        name="sample_attn",
    )(qa_g, _new_keys_t(ka), _new_keys_t(va), cache_a_k_t, cache_a_v_t, sink_g,
      qb_g, _new_keys_t(kb), _new_keys_t(vb), cache_b_k_t, cache_b_v_t)
    oa = oa.reshape(n, ga, reps, l, HEAD_DIM).transpose(0, 3, 2, 1, 4).reshape(n * l, A_Q)
    ob = ob[:, :, :l].transpose(0, 2, 1, 3).reshape(n * l, B_W)
    return oa, ob


def _rwkv_prep_kernel(z_ref, zfirst_ref, mu_ref, w0_ref, a0_ref, kk_ref, ka_ref, w2_ref, a2_ref, g2_ref, seg_ref,
                      r_out, k_out, v_out, al_out, b_out, lw_out, g_out, carry_ref, *, seq_len):
    tile = pl.program_id(1)
    z = z_ref[...]
    tm = z.shape[0]
    rolled = pltpu.roll(z, 1, axis=0)
    if seq_len >= tm:
        @pl.when(tile == 0)
        def _():
            carry_ref[...] = zfirst_ref[0]

        prev = jnp.where(_iota((tm, 1), 0) == 0, carry_ref[...], rolled)
        carry_ref[...] = z[tm - 1:tm, :]
    else:
        prev = jnp.where(_iota((tm, 1), 0) % seq_len == 0, zfirst_ref[...], rolled)
    zs = z + (prev - z) * mu_ref[...]
    r = zs[:, 0:C_W]
    k = zs[:, C_W:2 * C_W]
    v = zs[:, 2 * C_W:3 * C_W]
    lora = zs[:, 3 * C_W:3 * C_W + LANES]
    zg = zs[:, 3 * C_W + LANES:]
    w_log = -jax.nn.softplus(-(w0_ref[...] + _dot_x3(jnp.tanh(lora), w2_ref[...]))) - 0.5
    a = jax.nn.sigmoid(a0_ref[...] + _dot_x3(lora, a2_ref[...]))
    g = _dot_x3(jax.nn.sigmoid(zg), g2_ref[...])
    kk = k * kk_ref[...]
    kk = kk / jnp.maximum(jnp.sqrt(_dot_x2(kk * kk, seg_ref[...])), 1e-12)
    r_out[...] = r
    k_out[...] = k * (1.0 + (a - 1.0) * ka_ref[...])
    v_out[...] = v
    al_out[...] = -kk
    b_out[...] = kk * a
    lw_out[...] = -jnp.exp(w_log)
    g_out[...] = g


def rwkv_prep(zc, z_before, lw, n_seq):
    t = zc.shape[0]
    seq_len = t // n_seq
    tm = min(PREP_ROWS, t)
    if seq_len >= tm:
        grid = (n_seq, seq_len // tm)
        zfirst = z_before.reshape(n_seq, 1, C_PROJ)
        zfirst_spec = pl.BlockSpec((1, 1, C_PROJ), lambda b, i: (b, 0, 0))
    else:
        grid = (1, t // tm)
        zfirst = jnp.pad(z_before.reshape(n_seq, 1, C_PROJ), ((0, 0), (0, seq_len - 1), (0, 0))).reshape(t, C_PROJ)
        zfirst_spec = pl.BlockSpec((tm, C_PROJ), lambda b, i: (i, 0))
    tiles = grid[1]
    row = lambda b, i: (b * tiles + i, 0)
    const = lambda s: pl.BlockSpec(s, lambda b, i: (0, 0))
    vec = lambda a: a.reshape(1, -1)
    zero64 = jnp.zeros((HEAD_DIM, C_W), F32)
    w2p = jnp.concatenate([lw['c_w2'], zero64], axis=0)
    a2p = jnp.concatenate([zero64, lw['c_a2']], axis=0)
    out_spec = pl.BlockSpec((tm, C_W), row)
    return pl.pallas_call(
        functools.partial(_rwkv_prep_kernel, seq_len=seq_len),
        grid=grid,
        in_specs=[pl.BlockSpec((tm, C_PROJ), row),
                  zfirst_spec,
                  const((1, C_PROJ)), const((1, C_W)), const((1, C_W)), const((1, C_W)), const((1, C_W)),
                  const((LANES, C_W)), const((LANES, C_W)), const((LANES, C_W)), const((C_W, C_W))],
        out_specs=[out_spec] * 7,
        out_shape=[jax.ShapeDtypeStruct((t, C_W), F32)] * 7,
        scratch_shapes=[pltpu.VMEM((1, C_PROJ), F32)],
        compiler_params=_cparams(("parallel", "arbitrary")),
        name="rwkv_prep",
    )(zc, zfirst, vec(lw['c_mu']), vec(lw['c_w0']), vec(lw['c_a0']), vec(lw['c_k_k']), vec(lw['c_k_a']),
      w2p, a2p, lw['c_g2'], _seg_ones(C_W, F32))


SCAN_GROUP = 8
SCAN_PACKED_GROUP = 2


def _pair_block_diag(x, mask_b):
    xb = x.astype(BF16)
    return jnp.concatenate([xb, xb], axis=0) * mask_b


def _rwkv_scan_kernel(r_ref, k_ref, v_ref, al_ref, b_ref, lw_ref, g_ref, s0_ref, rk_ref, lnw_ref, lnb_ref, seg_ref,
                      o_ref, sfin_ref, s_ref, *, seq_len, group):
    step = pl.program_id(1)
    pairs = C_W // LANES
    rows = group * CHUNK
    chain = seq_len >= CHUNK
    span = min(seq_len, CHUNK)
    per = CHUNK // span
    ri = _iota((LANES, LANES), 0)
    ci = _iota((LANES, LANES), 1)
    bd_mask = (ri // HEAD_DIM) == (ci // HEAD_DIM)
    mask_b = jnp.where(bd_mask, 1.0, 0.0).astype(BF16)
    bd = lambda x: _pair_block_diag(x, mask_b)

    if chain:
        @pl.when(step == 0)
        def _():
            s_ref[...] = s0_ref[0]

    r, k, v, al, b, lw = r_ref[...], k_ref[...], v_ref[...], al_ref[...], b_ref[...], lw_ref[...]

    ti = _iota((rows, rows), 0)
    si = _iota((rows, rows), 1)
    same = ti // span == si // span
    tri = jnp.where((si <= ti) & same, 1.0, 0.0).astype(BF16)
    ones = jnp.where(same, 1.0, 0.0).astype(BF16)
    lw_hi, lw_lo = _split(lw)
    cum = jnp.dot(tri, lw_hi, preferred_element_type=F32) + jnp.dot(tri, lw_lo, preferred_element_type=F32)
    cum_end = jnp.dot(ones, lw_hi, preferred_element_type=F32) + jnp.dot(ones, lw_lo, preferred_element_type=F32)
    e_neg = jnp.exp(-cum)
    e_rem = jnp.exp(cum_end - cum)
    a_t = al * jnp.exp(cum - lw)
    b_t = b * e_neg
    k_t = k * e_neg
    r_t = r * jnp.exp(cum)
    b_h = b * e_rem
    k_h = k * e_rem

    tt = _iota((CHUNK, LANES), 0)
    ss = _iota((CHUNK, LANES), 1) % CHUNK
    together = tt // span == ss // span
    strict = (tt > ss) & together
    incl = (tt >= ss) & together
    eye = (tt == ss).astype(F32)
    items = [(c, m) for c in range(group) for m in range(pairs)]
    tile = lambda x, it: x[it[0] * CHUNK:(it[0] + 1) * CHUNK, it[1] * LANES:(it[1] + 1) * LANES]

    def independent(its, tick):
        gram = {}
        for it in its:
            ar = jnp.concatenate([tile(a_t, it), tile(r_t, it)], axis=0)
            bk = jnp.concatenate([bd(tile(b_t, it)), bd(tile(k_t, it))], axis=0)
            gram[it] = _dot_nt(ar, bk)
        tick()
        m_ab = {it: jnp.where(strict, gram[it][:CHUNK, :LANES], 0.0) for it in its}
        n_rb = {it: jnp.where(incl, gram[it][CHUNK:, :LANES], 0.0) for it in its}
        m_ak = {it: jnp.where(strict, gram[it][:CHUNK, LANES:], 0.0) for it in its}
        n_rk = {it: jnp.where(incl, gram[it][CHUNK:, LANES:], 0.0) for it in its}
        t_inv = {it: eye + jnp.where(tt // 2 == ss // 2, m_ab[it], 0.0) for it in its}
        blk = 4
        while blk <= span:
            joins = (tt // blk == ss // blk) & (tt // (blk // 2) != ss // (blk // 2))
            de = {it: _dot(t_inv[it], bd(jnp.where(joins, m_ab[it], 0.0))) for it in its}
            tick()
            t_inv = {it: t_inv[it] + _dot(de[it], bd(t_inv[it])) for it in its}
            tick()
            blk *= 2
        wv = {it: _dot(jnp.concatenate([m_ak[it], n_rk[it]], axis=0), bd(tile(v, it))) for it in its}
        tick()
        pq = {it: _dot(t_inv[it], jnp.concatenate([bd(tile(a_t, it)), bd(wv[it][:CHUNK])], axis=1)) for it in its}
        tick()
        return n_rb, wv, pq

    y = {}

    def dependent(chunks, parts, state):
        n_rb, wv, pq = parts
        heads = range(pairs)
        for c in chunks:
            uy = [_dot_nt(jnp.concatenate([pq[c, m][:, :LANES], tile(r_t, (c, m))], axis=0), state[m]) for m in heads]
            yield
            u = [uy[m][:CHUNK] + pq[c, m][:, LANES:] for m in heads]
            yd = [_dot(n_rb[c, m], bd(u[m])) for m in heads]
            upd = [_dot(jnp.concatenate([u[m], tile(v, (c, m))], axis=0).T,
                        jnp.concatenate([tile(b_h, (c, m)), tile(k_h, (c, m))], axis=0)) for m in heads]
            yield
            for m in heads:
                y[c, m] = uy[m][CHUNK:] + yd[m] + wv[c, m][CHUNK:]
                g_end = jnp.exp(tile(cum_end, (c, m))[0:1, :])
                state[m] = state[m] * g_end + jnp.where(bd_mask, upd[m], 0.0)
            yield

    if chain:
        state = [s_ref[m] for m in range(pairs)]
        halves = [range(group)] if group < 2 else [range(group // 2), range(group // 2, group)]
        pending = iter(())
        for half in halves:
            parts = independent([(c, m) for c in half for m in range(pairs)], lambda: next(pending, None))
            for _ in pending:
                pass
            pending = dependent(half, parts, state)
        for _ in pending:
            pass
        for m in range(pairs):
            s_ref[m] = state[m]
    else:
        n_rb, wv, pq = independent(items, lambda: None)
        row_seq = _iota((2 * CHUNK, 1), 0) % CHUNK // span
        col_seq = _iota((LANES, 2 * CHUNK), 1) % CHUNK // span
        for it in items:
            c, m = it
            pr = jnp.concatenate([pq[it][:, :LANES], tile(r_t, it)], axis=0)
            uy = None
            for j in range(per):
                part = _dot_nt(jnp.where(row_seq == j, pr, 0.0), s0_ref[c * per + j, m])
                uy = part if uy is None else uy + part
            u = uy[:CHUNK] + pq[it][:, LANES:]
            y[it] = uy[CHUNK:] + _dot(n_rb[it], bd(u)) + wv[it][CHUNK:]
            uv_t = jnp.concatenate([u, tile(v, it)], axis=0).T
            bk = jnp.concatenate([tile(b_h, it), tile(k_h, it)], axis=0).astype(BF16)
            for j in range(per):
                upd = _dot(jnp.where(col_seq == j, uv_t, 0.0), bk)
                g_end = jnp.exp(tile(cum_end, it)[j * span:j * span + 1, :])
                sfin_ref[c * per + j, m] = s0_ref[c * per + j, m] * g_end + jnp.where(bd_mask, upd, 0.0)

    seg = seg_ref[...]
    for m in range(pairs):
        sl = slice(m * LANES, (m + 1) * LANES)
        ym = jnp.concatenate([y[(c, m)] for c in range(group)], axis=0)
        mu = _dot_x2(ym, seg) * (1.0 / HEAD_DIM)
        dy = ym - mu
        var = _dot_x2(dy * dy, seg) * (1.0 / HEAD_DIM)
        yn = dy * lax.rsqrt(var + GN_EPS) * lnw_ref[:, sl] + lnb_ref[:, sl]
        bonus = _dot_x2(r[:, sl] * k[:, sl] * rk_ref[:, sl], seg)
        o_ref[:, sl] = (yn + bonus * v[:, sl]) * g_ref[:, sl]

    if chain:
        @pl.when(step == pl.num_programs(1) - 1)
        def _():
            sfin_ref[0] = s_ref[...]


def _state_to_pairs(s):
    n, h, hd, _ = s.shape
    s = s.reshape(n, h // 2, 2, hd, hd)
    bd = s[:, :, :, :, None, :] * jnp.eye(2, dtype=s.dtype)[None, None, :, None, :, None]
    return bd.reshape(n, h // 2, 2 * hd, 2 * hd)


def _pairs_to_state(bd):
    n, p, w, _ = bd.shape
    hd = w // 2
    bd = bd.reshape(n, p, 2, hd, 2, hd)
    return jnp.stack([bd[:, :, 0, :, 0, :], bd[:, :, 1, :, 1, :]], axis=2).reshape(n, 2 * p, hd, hd)


def rwkv_scan(prep, s0, lw, n_seq):
    t = prep[0].shape[0]
    seq_len = t // n_seq
    pairs = C_W // LANES
    if seq_len >= CHUNK:
        group = min(SCAN_GROUP, seq_len // CHUNK)
        grid = (n_seq, seq_len // CHUNK // group)
        n_state = 1
    else:
        group = min(SCAN_PACKED_GROUP, t // CHUNK)
        grid = (t // CHUNK // group, 1)
        n_state = group * (CHUNK // seq_len)
    steps = grid[1]
    row = pl.BlockSpec((group * CHUNK, C_W), lambda b, c: (b * steps + c, 0))
    const = lambda s: pl.BlockSpec(s, lambda b, c: (0, 0))
    state = pl.BlockSpec((n_state, pairs, LANES, LANES), lambda b, c: (b, 0, 0, 0))
    vec = lambda a: a.reshape(1, -1)
    out, s_fin = pl.pallas_call(
        functools.partial(_rwkv_scan_kernel, seq_len=seq_len, group=group),
        grid=grid,
        in_specs=[row] * 7 + [state, const((1, C_W)), const((1, C_W)), const((1, C_W)), const((LANES, LANES))],
        out_specs=[row, state],
        out_shape=[jax.ShapeDtypeStruct((t, C_W), F32), jax.ShapeDtypeStruct((n_seq, pairs, LANES, LANES), F32)],
        scratch_shapes=[pltpu.VMEM((pairs, LANES, LANES), F32)],
        compiler_params=_cparams(("parallel", "arbitrary")),
        name="rwkv_scan",
    )(*prep, _state_to_pairs(s0), vec(lw['c_r_k']), vec(lw['c_ln_w']), vec(lw['c_ln_b']), _seg_ones(LANES, F32))
    return out, _pairs_to_state(s_fin)


FF_CHUNK = 1024


def _rms(x, g):
    return x * lax.rsqrt(jnp.mean(x * x, axis=-1, keepdims=True) + RMS_EPS) * g


def _mix_mlp_kernel(x_ref, oa_ref, ob_ref, oc_ref, wout_ref, g_ref, up_ref, down_ref, gf_ref, y_ref, xm_ref,
                    *, final):
    mix = jnp.concatenate([oa_ref[...], ob_ref[...], oc_ref[...]], axis=1).astype(BF16)
    xm_ref[...] = x_ref[...] + jnp.dot(mix, wout_ref[...], preferred_element_type=F32)
    h = _rms(xm_ref[...], g_ref[...]).astype(BF16)
    acc = xm_ref[...]
    for c in range(up_ref.shape[1] // FF_CHUNK):
        u = jnp.dot(h, up_ref[:, c * FF_CHUNK:(c + 1) * FF_CHUNK], preferred_element_type=F32)
        u = jnp.square(jnp.maximum(u, 0.0)).astype(BF16)
        acc += jnp.dot(u, down_ref[c * FF_CHUNK:(c + 1) * FF_CHUNK, :], preferred_element_type=F32)
    y_ref[...] = _rms(acc, gf_ref[...]) if final else acc


def mix_mlp(x, oa, ob, oc, w_out_b, g, up_b, down_b, g_final, final, tm):
    t, d = x.shape
    row = lambda w: pl.BlockSpec((tm, w), lambda i: (i, 0))
    vec = lambda: pl.BlockSpec((1, d), lambda i: (0, 0))
    weight = lambda w: pl.BlockSpec(w.shape, lambda i: (0, 0), pipeline_mode=pl.Buffered(1))
    return pl.pallas_call(
        functools.partial(_mix_mlp_kernel, final=final), grid=(t // tm,),
        in_specs=[row(d), row(A_Q), row(B_W), row(C_W), pl.BlockSpec(w_out_b.shape, lambda i: (0, 0)), vec(),
                  weight(up_b), weight(down_b), vec()],
        out_specs=row(d), out_shape=jax.ShapeDtypeStruct((t, d), F32),
        scratch_shapes=[pltpu.VMEM((tm, d), F32)],
        compiler_params=_cparams(("parallel",)), name="mix_mlp",
    )(x, oa, ob, oc, w_out_b, g.reshape(1, d), up_b, down_b, g_final.reshape(1, d))


PREP_ROWS = 256
A_PERM = (0, 2, 1, 3)


def _permute_a_heads(w, axis):
    take = lambda a, b: lax.slice_in_dim(w, a, b, axis=axis)
    parts = [take(h * HEAD_DIM, (h + 1) * HEAD_DIM) for h in A_PERM] + [take(A_Q, w.shape[axis])]
    return jnp.concatenate(parts, axis=axis)


def _row_tile(t):
    return 512 if t % 512 == 0 else t


def trunk_layer(x, lw, n, past, layer):
    t = x.shape[0]
    l = t // n
    tm = _row_tile(t)
    if past is None:
        qa, ka, va, qb, kb, vb, zc, *tails = norm_proj(x, lw['norm1_g'], lw['w_in'], n, True)
        oa = band_attn(qa, ka, va, lw['sink'], n=n, kv_of_q=(0, 0))
        ob = dilated_attn(qb, kb, vb, n)
        z_before = jnp.zeros((n, C_PROJ), F32)
        s0 = jnp.zeros((n, C_W // HEAD_DIM, HEAD_DIM, HEAD_DIM), F32)
        windows = [a.reshape(n, -1, HEAD_DIM, a.shape[2]).transpose(0, 3, 1, 2) for a in tails]
    else:
        qa, ka, va, qb, kb, vb, zc = norm_proj(x, lw['norm1_g'], lw['w_in'], 1, False)
        cache_a_k_t, cache_a_v_t, cache_b_k_t, cache_b_v_t, state_wkv, state_shift = past
        oa, ob = sample_attn(qa, ka, va, cache_a_k_t, cache_a_v_t, lw['a_sink'],
                             qb, kb, vb, cache_b_k_t, cache_b_v_t, layer, n, l)
        z_before, s0 = state_shift[layer], state_wkv[layer]
        windows = [a.reshape(n, l, -1, HEAD_DIM) for a in (ka, va, kb, vb)]
    oc, s_fin = rwkv_scan(rwkv_prep(zc, z_before, lw, n), s0, lw, n)
    x = mix_mlp(x, oa, ob, oc, lw['w_out'], lw['norm2_g'], lw['mlp_up'], lw['mlp_down'], lw['final_norm_g'],
                lw['last'], tm)
    return x, (*windows, s_fin, zc.reshape(n, l, C_PROJ)[:, -1])


def kernel(x_prompt, x_sample, cache_a_k, cache_a_v, cache_b_k, cache_b_v, state_c_wkv, state_c_shift, norm1_g, norm2_g, w_in, w_out, a_sink, c_mu, c_w0, c_w2, c_a0, c_a2, c_g2, c_k_k, c_k_a, c_r_k, c_ln_w, c_ln_b, mlp_up, mlp_down, final_norm_g):
    depth = w_in.shape[0]
    n_p, l_p, d_model = x_prompt.shape
    n_s, l_s, _ = x_sample.shape
    w_in_b = _permute_a_heads(w_in, 2).astype(BF16)
    w_out_b = _permute_a_heads(w_out, 1).astype(BF16)
    up_b = mlp_up.astype(BF16)
    down_b = mlp_down.astype(BF16)
    sink = jnp.repeat(a_sink[:, jnp.array(A_PERM)], HEAD_DIM, axis=1).reshape(depth, 1, A_Q)
    cache_t = lambda c: jnp.transpose(c, (0, 1, 3, 4, 2))
    past = (cache_t(cache_a_k), cache_t(cache_a_v), cache_t(cache_b_k), cache_t(cache_b_v), state_c_wkv, state_c_shift)

    yp = x_prompt.reshape(n_p * l_p, d_model)
    ys = x_sample.reshape(n_s * l_s, d_model)
    p_new, s_new = [], []
    for l in range(depth):
        lw = {'norm1_g': norm1_g[l], 'norm2_g': norm2_g[l], 'w_in': w_in_b[l], 'w_out': w_out_b[l], 'sink': sink[l], 'a_sink': a_sink[l],
              'final_norm_g': final_norm_g, 'last': l == depth - 1,
              'c_mu': c_mu[l], 'c_w0': c_w0[l], 'c_w2': c_w2[l], 'c_a0': c_a0[l], 'c_a2': c_a2[l], 'c_g2': c_g2[l],
              'c_k_k': c_k_k[l], 'c_k_a': c_k_a[l], 'c_r_k': c_r_k[l], 'c_ln_w': c_ln_w[l], 'c_ln_b': c_ln_b[l],
              'mlp_up': up_b[l], 'mlp_down': down_b[l]}
        yp, st_p = trunk_layer(yp, lw, n_p, None, l)
        ys, st_s = trunk_layer(ys, lw, n_s, past, l)
        p_new.append(st_p)
        s_new.append(st_s)
    p_state = [jnp.stack(t) for t in zip(*p_new)]
    s_state = [jnp.stack(t) for t in zip(*s_new)]
    return (yp.reshape(n_p, l_p, d_model), ys.reshape(n_s, l_s, d_model), *p_state, *s_state)
```
